```python
import jax, jax.numpy as jnp
from jax import lax
import numpy as np

D_MODEL = 1024
BATCH = 16
SEQ = 2048
DEPTH = 1
DEC_BATCH = 16
DEC_SEQ = 16
PAST_LEN = 1024

CHUNK = 64
Q_BLOCK = 128
EPS = 1e-6
M_HEAD_DIM = 64
M_INNER = D_MODEL
M_HEADS = M_INNER // M_HEAD_DIM
M_GROUPS = 4
M_HPG = M_HEADS // M_GROUPS
M_STATE = 128
CONV_W = 4
CONV_DIM = M_INNER + 2 * M_GROUPS * M_STATE
SB_HEAD_DIM = 64
SB_HEADS = D_MODEL // SB_HEAD_DIM
SB_WIDTH = SB_HEADS * SB_HEAD_DIM
SB_SCALE = SB_HEAD_DIM ** -0.5
D_FF = -(-8 * D_MODEL // (3 * 256)) * 256
PLE_DIM = 256
OFF_Z = 0
OFF_XBC = OFF_Z + M_INNER
OFF_DT = OFF_XBC + CONV_DIM
OFF_Q = OFF_DT + M_HEADS
OFF_K = OFF_Q + SB_WIDTH
OFF_V = OFF_K + SB_WIDTH
OFF_GA = OFF_V + SB_WIDTH
OFF_GB = OFF_GA + D_MODEL
IN_COLS = OFF_GB + D_MODEL

kernel_name = "ssd_stickbreaking_gated_hybrid_step"


def rmsnorm(x, g):
    xf = x.astype(jnp.float32)
    xf = xf * lax.rsqrt(jnp.mean(xf * xf, axis=-1, keepdims=True) + EPS)
    return (xf * g.astype(jnp.float32)).astype(x.dtype)


def ssd_scan(x, dt, a_head, b_mat, c_mat, h0):
    bsz, L = x.shape[0], x.shape[1]
    q = L if L <= CHUNK else CHUNK
    nc = L // q
    x = x.reshape(bsz, nc, q, M_GROUPS, M_HPG, M_HEAD_DIM)
    dt = dt.reshape(bsz, nc, q, M_GROUPS, M_HPG)
    bm = b_mat.reshape(bsz, nc, q, M_GROUPS, M_STATE)
    cm = c_mat.reshape(bsz, nc, q, M_GROUPS, M_STATE)
    acum = jnp.cumsum(dt * a_head, axis=2)
    causal = jnp.tril(jnp.ones((q, q), dtype=bool))[None, None, :, :, None, None]
    seg = acum[:, :, :, None] - acum[:, :, None, :]
    decay = jnp.exp(jnp.where(causal, seg, -jnp.inf))
    cb = jnp.einsum('bclgn,bcsgn->bclsg', cm, bm)
    y_diag = jnp.einsum('bclsg,bclsgr,bcsgr,bcsgrp->bclgrp', cb, decay, dt, x)
    decay_end = jnp.exp(acum[:, :, -1:] - acum)
    states = jnp.einsum('bclgn,bclgr,bclgrp->bcgrpn', bm, decay_end * dt, x)
    chunk_decay = jnp.exp(acum[:, :, -1])

    def step(h, inp):
        s_c, d_c = inp
        return d_c[..., None, None] * h + s_c, h

    h_last, h_in = lax.scan(step, h0, (jnp.moveaxis(states, 1, 0), jnp.moveaxis(chunk_decay, 1, 0)))
    h_in = jnp.moveaxis(h_in, 0, 1)
    y_off = jnp.einsum('bclgn,bcgrpn,bclgr->bclgrp', cm, h_in, jnp.exp(acum))
    y = (y_diag + y_off).reshape(bsz, L, M_GROUPS, M_HPG, M_HEAD_DIM)
    return y, h_last


def sb_block(qb, qpos, k, v):
    kpos = jnp.arange(k.shape[2])
    z = jnp.einsum('bhtd,bhsd->bhts', qb, k) * SB_SCALE
    mask = (kpos[None, :] < qpos[:, None])[None, None]
    log_keep = jnp.where(mask, jax.nn.log_sigmoid(-z), 0.0)
    after = lax.cumsum(log_keep, axis=3, reverse=True) - log_keep
    w = jnp.where(mask, jnp.exp(jax.nn.log_sigmoid(z) + after), 0.0)
    return jnp.einsum('bhts,bhsd->bhtd', w, v)


def sb_attention(q, k, v, q_offset):
    bsz, nh, lq, hd = q.shape
    qpos = q_offset + jnp.arange(lq)
    if lq <= Q_BLOCK:
        return sb_block(q, qpos, k, v)
    nb = lq // Q_BLOCK
    qb = jnp.moveaxis(q.reshape(bsz, nh, nb, Q_BLOCK, hd), 2, 0)
    out = lax.map(lambda a: sb_block(a[0], a[1], k, v), (qb, qpos.reshape(nb, Q_BLOCK)))
    return jnp.moveaxis(out, 0, 2).reshape(bsz, nh, lq, hd)


def hybrid_layer(x, p, past_k, past_v, conv_hist, h0, norm_mix_g, w_in, conv_w, conv_b, dt_bias,
                 a_log, d_skip, mnorm_g, w_a, w_b, w_out, norm_ffn_g, w_gate, w_up, w_down,
                 norm_ple_g, w_ple_gate, w_ple):
    f32 = jnp.float32
    bsz, L, _ = x.shape
    h = rmsnorm(x, norm_mix_g)
    proj = h @ w_in
    z = proj[..., OFF_Z:OFF_XBC]
    xbc_raw = proj[..., OFF_XBC:OFF_DT]
    dt_raw = proj[..., OFF_DT:OFF_Q]
    q = proj[..., OFF_Q:OFF_K].reshape(bsz, L, SB_HEADS, SB_HEAD_DIM).transpose(0, 2, 1, 3)
    k = proj[..., OFF_K:OFF_V].reshape(bsz, L, SB_HEADS, SB_HEAD_DIM).transpose(0, 2, 1, 3)
    v = proj[..., OFF_V:OFF_GA].reshape(bsz, L, SB_HEADS, SB_HEAD_DIM).transpose(0, 2, 1, 3)
    gate_a = proj[..., OFF_GA:OFF_GB]
    gate_b = proj[..., OFF_GB:IN_COLS]

    xpad = jnp.concatenate([conv_hist.astype(xbc_raw.dtype), xbc_raw], axis=1)
    new_conv = xpad[:, xpad.shape[1] - (CONV_W - 1):]
    xbc = lax.conv_general_dilated(xpad, conv_w[:, None, :].astype(xpad.dtype), (1,), 'VALID',
                                   dimension_numbers=('NWC', 'WIO', 'NWC'),
                                   feature_group_count=CONV_DIM) + conv_b
    xbc = jax.nn.silu(xbc.astype(f32))
    xs = xbc[..., :M_INNER].reshape(bsz, L, M_GROUPS, M_HPG, M_HEAD_DIM)
    bm = xbc[..., M_INNER:M_INNER + M_GROUPS * M_STATE].reshape(bsz, L, M_GROUPS, M_STATE)
    cm = xbc[..., M_INNER + M_GROUPS * M_STATE:].reshape(bsz, L, M_GROUPS, M_STATE)
    dt = jax.nn.softplus(dt_raw.astype(f32) + dt_bias.astype(f32)).reshape(bsz, L, M_GROUPS, M_HPG)
    a_head = -jnp.exp(a_log.astype(f32)).reshape(M_GROUPS, M_HPG)
    h0g = h0.astype(f32).reshape(bsz, M_GROUPS, M_HPG, M_HEAD_DIM, M_STATE)
    y, h_last = ssd_scan(xs, dt, a_head, bm, cm, h0g)
    y = y + d_skip.astype(f32).reshape(M_GROUPS, M_HPG)[:, :, None] * xs
    y = (y.reshape(bsz, L, M_INNER) * jax.nn.silu(z.astype(f32))).reshape(bsz, L, M_GROUPS, M_INNER // M_GROUPS)
    y = y * lax.rsqrt(jnp.mean(y * y, axis=-1, keepdims=True) + EPS)
    y = y.reshape(bsz, L, M_INNER) * mnorm_g.astype(f32)
    branch_a = y.astype(x.dtype) @ w_a
    new_ssm = h_last.reshape(bsz, M_HEADS, M_HEAD_DIM, M_STATE)

    k_all = jnp.concatenate([past_k.astype(k.dtype), k], axis=2)
    v_all = jnp.concatenate([past_v.astype(v.dtype), v], axis=2)
    o = sb_attention(q.astype(f32), k_all.astype(f32), v_all.astype(f32), past_k.shape[2])
    o = o.transpose(0, 2, 1, 3).reshape(bsz, L, SB_WIDTH).astype(x.dtype)
    branch_b = o @ w_b

    merged = jax.nn.sigmoid(gate_a) * branch_a + jax.nn.sigmoid(gate_b) * branch_b
    x = x + merged @ w_out

    hf = rmsnorm(x, norm_ffn_g)
    x = x + (jax.nn.silu(hf @ w_gate) * (hf @ w_up)) @ w_down

    g = jax.nn.sigmoid(rmsnorm(x, norm_ple_g) @ w_ple_gate)
    x = x + g * (p.astype(x.dtype) @ w_ple)
    return x, k, v, new_conv, new_ssm


def setup_inputs(seed: int = 0) -> dict:
    key = jax.random.key(seed)
    ks = jax.random.split(key, 32)
    nrm = lambda k, s, sc: jax.random.normal(k, s, jnp.float32) * sc
    dt0 = jnp.exp(jax.random.uniform(ks[12], (DEPTH, M_HEADS), jnp.float32, np.log(1e-3), np.log(1e-1)))
    return {
        "x_prompt": nrm(ks[0], (BATCH, SEQ, D_MODEL), 1.0),
        "x_sample": nrm(ks[1], (DEC_BATCH, DEC_SEQ, D_MODEL), 1.0),
        "cache_k": nrm(ks[2], (DEPTH, DEC_BATCH, SB_HEADS, PAST_LEN, SB_HEAD_DIM), 1.0),
        "cache_v": nrm(ks[3], (DEPTH, DEC_BATCH, SB_HEADS, PAST_LEN, SB_HEAD_DIM), 1.0),
        "state_conv": nrm(ks[4], (DEPTH, DEC_BATCH, CONV_W - 1, CONV_DIM), 1.0),
        "state_ssm": nrm(ks[5], (DEPTH, DEC_BATCH, M_HEADS, M_HEAD_DIM, M_STATE), 0.1),
        "p_prompt": nrm(ks[6], (DEPTH, BATCH, SEQ, PLE_DIM), 1.0),
        "p_sample": nrm(ks[7], (DEPTH, DEC_BATCH, DEC_SEQ, PLE_DIM), 1.0),
        "norm_mix_g": 1.0 + nrm(ks[8], (DEPTH, D_MODEL), 0.02),
        "w_in": nrm(ks[9], (DEPTH, D_MODEL, IN_COLS), D_MODEL ** -0.5),
        "conv_w": nrm(ks[10], (DEPTH, CONV_W, CONV_DIM), CONV_W ** -0.5),
        "conv_b": nrm(ks[11], (DEPTH, CONV_DIM), 0.02),
        "dt_bias": dt0 + jnp.log(-jnp.expm1(-dt0)),
        "a_log": jnp.log(jax.random.uniform(ks[13], (DEPTH, M_HEADS), jnp.float32, 1.0, 16.0)),
        "d_skip": 1.0 + nrm(ks[14], (DEPTH, M_HEADS), 0.1),
        "mnorm_g": 1.0 + nrm(ks[15], (DEPTH, M_INNER), 0.02),
        "w_a": nrm(ks[16], (DEPTH, M_INNER, D_MODEL), M_INNER ** -0.5),
        "w_b": nrm(ks[17], (DEPTH, SB_WIDTH, D_MODEL), SB_WIDTH ** -0.5),
        "w_out": nrm(ks[18], (DEPTH, D_MODEL, D_MODEL), D_MODEL ** -0.5),
        "norm_ffn_g": 1.0 + nrm(ks[19], (DEPTH, D_MODEL), 0.02),
        "w_gate": nrm(ks[20], (DEPTH, D_MODEL, D_FF), D_MODEL ** -0.5),
        "w_up": nrm(ks[21], (DEPTH, D_MODEL, D_FF), D_MODEL ** -0.5),
        "w_down": nrm(ks[22], (DEPTH, D_FF, D_MODEL), D_FF ** -0.5),
        "norm_ple_g": 1.0 + nrm(ks[23], (DEPTH, D_MODEL), 0.02),
        "w_ple_gate": nrm(ks[24], (DEPTH, D_MODEL, D_MODEL), D_MODEL ** -0.5),
        "w_ple": nrm(ks[25], (DEPTH, PLE_DIM, D_MODEL), PLE_DIM ** -0.5),
        "final_norm_g": 1.0 + nrm(ks[26], (D_MODEL,), 0.02),
    }


def reference(x_prompt, x_sample, cache_k, cache_v, state_conv, state_ssm, p_prompt, p_sample,
              norm_mix_g, w_in, conv_w, conv_b, dt_bias, a_log, d_skip, mnorm_g, w_a, w_b, w_out,
              norm_ffn_g, w_gate, w_up, w_down, norm_ple_g, w_ple_gate, w_ple, final_norm_g):
    bp = x_prompt.shape[0]
    xp, xs = x_prompt, x_sample
    kp_l, vp_l, cp_l, sp_l = [], [], [], []
    ks_l, vs_l, cs_l, ss_l = [], [], [], []
    for i in range(DEPTH):
        w = (norm_mix_g[i], w_in[i], conv_w[i], conv_b[i], dt_bias[i], a_log[i], d_skip[i], mnorm_g[i],
             w_a[i], w_b[i], w_out[i], norm_ffn_g[i], w_gate[i], w_up[i], w_down[i],
             norm_ple_g[i], w_ple_gate[i], w_ple[i])
        empty_kv = jnp.zeros((bp, SB_HEADS, 0, SB_HEAD_DIM), xp.dtype)
        zero_conv = jnp.zeros((bp, CONV_W - 1, CONV_DIM), xp.dtype)
        zero_ssm = jnp.zeros((bp, M_HEADS, M_HEAD_DIM, M_STATE), jnp.float32)
        xp, kp, vp, cp, sp = hybrid_layer(xp, p_prompt[i], empty_kv, empty_kv, zero_conv, zero_ssm, *w)
        xs, k_s, v_s, c_s, s_s = hybrid_layer(xs, p_sample[i], cache_k[i], cache_v[i], state_conv[i], state_ssm[i], *w)
        kp_l.append(kp); vp_l.append(vp); cp_l.append(cp); sp_l.append(sp)
        ks_l.append(k_s); vs_l.append(v_s); cs_l.append(c_s); ss_l.append(s_s)
    y_prompt = rmsnorm(xp, final_norm_g)
    y_sample = rmsnorm(xs, final_norm_g)
    return (y_prompt, y_sample,
            jnp.stack(kp_l), jnp.stack(vp_l), jnp.stack(cp_l), jnp.stack(sp_l),
            jnp.stack(ks_l), jnp.stack(vs_l), jnp.stack(cs_l), jnp.stack(ss_l))
```

```python
import functools

import jax
import jax.numpy as jnp
from jax import lax
from jax.experimental import pallas as pl
from jax.experimental.pallas import tpu as pltpu

F32 = jnp.float32
BF16 = jnp.bfloat16

EPS = 1e-6
D_MODEL = 1024
CHUNK = 64
M_HEAD_DIM = 64
M_HEADS = 16
M_GROUPS = 4
M_STATE = 128
M_INNER = M_HEADS * M_HEAD_DIM
CONV_W = 4
CONV_DIM = M_INNER + 2 * M_GROUPS * M_STATE
SB_HEADS = 16
SB_HEAD_DIM = 64
SB_WIDTH = SB_HEADS * SB_HEAD_DIM
SB_SCALE = SB_HEAD_DIM ** -0.5
D_FF = 2816
PLE_DIM = 256
OFF_Z = 0
OFF_XBC = OFF_Z + M_INNER
OFF_DT = OFF_XBC + CONV_DIM
OFF_Q = OFF_DT + M_HEADS
OFF_K = OFF_Q + SB_WIDTH
OFF_V = OFF_K + SB_WIDTH
OFF_GA = OFF_V + SB_WIDTH
OFF_GB = OFF_GA + D_MODEL
IN_COLS = OFF_GB + D_MODEL

LANES = 128
DT_PAD = LANES
P_Z = 0
P_XBC = P_Z + M_INNER
P_DT = P_XBC + CONV_DIM
P_Q = P_DT + DT_PAD
P_K = P_Q + SB_WIDTH
P_V = P_K + SB_WIDTH
P_GA = P_V + SB_WIDTH
P_GB = P_GA + D_MODEL
P_COLS = P_GB + D_MODEL

VMEM_LIMIT = 56 * 1024 * 1024
SB_LOG_CUTOFF = -104.0


def _const_spec(shape):
    nd = len(shape)
    return pl.BlockSpec(shape, lambda *_: (0,) * nd, pipeline_mode=pl.Buffered(1))


def _rms(x, g):
    ms = jnp.mean(x * x, axis=-1, keepdims=True)
    return x * lax.rsqrt(ms + EPS) * g


def _split2(x):
    hi = x.astype(BF16)
    lo = (x - hi.astype(F32)).astype(BF16)
    return hi, lo


def _split3(x):
    hi = x.astype(BF16)
    r = x - hi.astype(F32)
    mid = r.astype(BF16)
    lo = (r - mid.astype(F32)).astype(BF16)
    return hi, mid, lo


def _dot(a, b):
    return jnp.dot(a, b, preferred_element_type=F32)


def _dot_nt(a, b):
    return lax.dot_general(a, b, (((1,), (1,)), ((), ())), preferred_element_type=F32)


def _dot_tn(a, b):
    return lax.dot_general(a, b, (((0,), (0,)), ((), ())), preferred_element_type=F32)


def _inproj_kernel(x_ref, g_ref, w_ref, z_ref, xbc_ref, dt_ref, q_ref, k_ref, v_ref, ko_ref, vo_ref,
                   ga_ref, gb_ref, *, nb, tl):
    tm = nb * tl
    x = x_ref[...].reshape(tm, D_MODEL)
    h = _rms(x, g_ref[...]).astype(BF16)

    def proj(lo, n):
        return _dot(h, w_ref[:, lo:lo + n])

    z_ref[...] = proj(P_Z, M_INNER).reshape(nb, tl, M_INNER)
    xbc_ref[...] = proj(P_XBC, CONV_DIM).reshape(nb, tl, CONV_DIM)
    dt_ref[...] = proj(P_DT, DT_PAD).reshape(nb, tl, DT_PAD)
    q_ref[...] = proj(P_Q, SB_WIDTH).astype(BF16).reshape(nb, tl, SB_WIDTH)
    for src, bf_ref, out_ref in ((P_K, k_ref, ko_ref), (P_V, v_ref, vo_ref)):
        kv = proj(src, SB_WIDTH)
        bf_ref[...] = kv.astype(BF16).reshape(nb, tl, SB_WIDTH)
        for b in range(nb):
            for hd in range(SB_HEADS):
                out_ref[b, hd] = kv[b * tl:(b + 1) * tl, hd * SB_HEAD_DIM:(hd + 1) * SB_HEAD_DIM]
    ga_ref[...] = proj(P_GA, D_MODEL).reshape(nb, tl, D_MODEL)
    gb_ref[...] = proj(P_GB, D_MODEL).reshape(nb, tl, D_MODEL)


def _inproj(x, g, w, *, nb, tl):
    bsz, L, _ = x.shape
    grid = (bsz // nb, L // tl)

    def tok(n):
        return pl.BlockSpec((nb, tl, n), lambda b, t: (b, t, 0))

    head = pl.BlockSpec((nb, SB_HEADS, tl, SB_HEAD_DIM), lambda b, t: (b, 0, t, 0))
    out_shape = (
        jax.ShapeDtypeStruct((bsz, L, M_INNER), F32),
        jax.ShapeDtypeStruct((bsz, L, CONV_DIM), F32),
        jax.ShapeDtypeStruct((bsz, L, DT_PAD), F32),
        jax.ShapeDtypeStruct((bsz, L, SB_WIDTH), BF16),
        jax.ShapeDtypeStruct((bsz, L, SB_WIDTH), BF16),
        jax.ShapeDtypeStruct((bsz, L, SB_WIDTH), BF16),
        jax.ShapeDtypeStruct((bsz, SB_HEADS, L, SB_HEAD_DIM), F32),
        jax.ShapeDtypeStruct((bsz, SB_HEADS, L, SB_HEAD_DIM), F32),
        jax.ShapeDtypeStruct((bsz, L, D_MODEL), F32),
        jax.ShapeDtypeStruct((bsz, L, D_MODEL), F32),
    )
    out_specs = (tok(M_INNER), tok(CONV_DIM), tok(DT_PAD), tok(SB_WIDTH), tok(SB_WIDTH), tok(SB_WIDTH),
                 head, head, tok(D_MODEL), tok(D_MODEL))
    return pl.pallas_call(
        functools.partial(_inproj_kernel, nb=nb, tl=tl),
        grid=grid,
        in_specs=[tok(D_MODEL), _const_spec((1, D_MODEL)), _const_spec((D_MODEL, P_COLS))],
        out_specs=out_specs,
        out_shape=out_shape,
        compiler_params=pltpu.CompilerParams(
            dimension_semantics=("parallel", "parallel"), vmem_limit_bytes=VMEM_LIMIT),
        name="inproj",
    )(x, g, w)


HIST_ROW = 8 - (CONV_W - 1)
PAIRS = M_HEADS // 2


def _ssd_kernel(xbc_ref, z_ref, dt_ref, hist_ref, h0_ref, cw_ref, cb_ref, dtb_ref, alog_ref, dsk_ref, mg_ref,
                ex_ref, y_ref, nconv_ref, nssm_ref, xp_s, xc_s, dt_s, da_s, h_s, *, q, tl):
    t = pl.program_id(1)
    nt = pl.num_programs(1)
    nchunk = tl // q

    @pl.when(t == 0)
    def _():
        xp_s[0:8, :] = jnp.zeros((8, CONV_DIM), F32)
        xp_s[HIST_ROW:8, :] = hist_ref[0]
        for i in range(PAIRS):
            h_s[:, i * LANES:(i + 1) * LANES] = h0_ref[0, i].T

    xp_s[8:8 + tl, :] = xbc_ref[0]
    xc = cb_ref[...] + xp_s[HIST_ROW:HIST_ROW + tl, :] * cw_ref[0:1, :]
    for w in range(1, CONV_W):
        xc = xc + xp_s[HIST_ROW + w:HIST_ROW + w + tl, :] * cw_ref[w:w + 1, :]
    xc_s[...] = xc * jax.nn.sigmoid(xc)
    tail = xp_s[8 + tl - (CONV_W - 1):8 + tl, :]

    @pl.when(t == nt - 1)
    def _():
        nconv_ref[0] = tail

    xp_s[HIST_ROW:8, :] = tail

    xdt = dt_ref[0] + dtb_ref[...]
    dt = jnp.maximum(xdt, 0.0) + jnp.log1p(jnp.exp(-jnp.abs(xdt)))
    dt_s[...] = dt
    da_s[...] = dt * (-jnp.exp(alog_ref[...]))

    ri = lax.broadcasted_iota(jnp.int32, (q, q), 0)
    ci = lax.broadcasted_iota(jnp.int32, (q, q), 1)
    causal = ri >= ci
    tri = causal.astype(BF16)
    lane = lax.broadcasted_iota(jnp.int32, (q, LANES), 1)
    even_head = lane < M_HEAD_DIM
    ex = ex_ref[...]
    pad_rows = LANES - q

    def transpose_cols(a):
        if pad_rows:
            a = jnp.concatenate([a, jnp.zeros((pad_rows, LANES), F32)], axis=0)
        return a.T[:, :q]

    def chunk(c, carry):
        r0 = pl.multiple_of(c * q, q)
        rows = pl.ds(r0, q)
        xs = xc_s[rows, 0:M_INNER]
        bm = xc_s[rows, M_INNER:M_INNER + M_GROUPS * M_STATE].astype(BF16)
        cm = xc_s[rows, M_INNER + M_GROUPS * M_STATE:CONV_DIM].astype(BF16)
        dtc = dt_s[rows, :]
        dac = da_s[rows, :]

        acum = sum(_dot(tri, p) for p in _split3(dac))
        tot = acum[q - 1:q, :]
        ey = jnp.exp(acum)
        ws = jnp.exp(tot - acum) * dtc
        dc = jnp.broadcast_to(jnp.exp(tot), (8, DT_PAD))
        stack = jnp.concatenate([ey, ws, dc], axis=0)
        stack_c = sum(_dot(p, ex) for p in _split2(stack))
        ey_c = stack_c[0:q]
        ws_c = stack_c[q:2 * q]
        dc_c = stack_c[2 * q:2 * q + 1]

        acum_t = transpose_cols(acum)
        dt_t = transpose_cols(dtc)

        xs_bf = xs.astype(BF16)
        xw_bf = (xs * ws_c).astype(BF16)
        h_prev = h_s[...]
        h_bf = h_prev.astype(BF16)

        ydiag = []
        yoff = []
        states = []
        for g in range(M_GROUPS):
            bg = bm[:, g * M_STATE:(g + 1) * M_STATE]
            cg = cm[:, g * M_STATE:(g + 1) * M_STATE]
            gcols = slice(g * 4 * M_HEAD_DIM, (g + 1) * 4 * M_HEAD_DIM)
            cb = _dot_nt(cg, bg)
            yoff.append(_dot(cg, h_bf[:, gcols]))
            states.append(_dot_tn(bg, xw_bf[:, gcols]))
            for pr in range(2):
                pair = []
                for r in range(2):
                    hd = g * 4 + pr * 2 + r
                    acol = jnp.broadcast_to(acum[:, hd:hd + 1], (q, q))
                    arow = jnp.broadcast_to(acum_t[hd:hd + 1, :], (q, q))
                    dec = jnp.exp(jnp.where(causal, acol - arow, -1e30))
                    m = (cb * dec * jnp.broadcast_to(dt_t[hd:hd + 1, :], (q, q))).astype(BF16)
                    pcols = slice((g * 2 + pr) * LANES, (g * 2 + pr + 1) * LANES)
                    pair.append(_dot(m, xs_bf[:, pcols]))
                ydiag.append(jnp.where(even_head, pair[0], pair[1]))
        y = jnp.concatenate(ydiag, axis=1) + jnp.concatenate(yoff, axis=1) * ey_c + dsk_ref[...] * xs
        zc = z_ref[0, rows, :]
        y = y * (zc * jax.nn.sigmoid(zc))
        gw = M_INNER // M_GROUPS
        normed = []
        for g in range(M_GROUPS):
            yg = y[:, g * gw:(g + 1) * gw]
            normed.append(yg * lax.rsqrt(jnp.mean(yg * yg, axis=-1, keepdims=True) + EPS))
        y = jnp.concatenate(normed, axis=1) * mg_ref[...]
        y_ref[0, rows, :] = y.astype(BF16)
        h_s[...] = h_prev * dc_c + jnp.concatenate(states, axis=1)
        return carry

    lax.fori_loop(0, nchunk, chunk, 0)

    @pl.when(t == nt - 1)
    def _():
        for i in range(PAIRS):
            nssm_ref[0, i] = h_s[:, i * LANES:(i + 1) * LANES].T


def _ssd(xbc, z, dt, hist, h0, cw, cb, dtb, alog, dsk, mg, ex, *, q, tl):
    bsz, L, _ = xbc.shape
    grid = (bsz, L // tl)
    h0p = h0.reshape(bsz, PAIRS, LANES, M_STATE)

    def tok(n):
        return pl.BlockSpec((1, tl, n), lambda b, t: (b, t, 0))

    per_b3 = pl.BlockSpec((1, CONV_W - 1, CONV_DIM), lambda b, t: (b, 0, 0))
    per_b4 = pl.BlockSpec((1, PAIRS, LANES, M_STATE), lambda b, t: (b, 0, 0, 0))
    y, nconv, nssm = pl.pallas_call(
        functools.partial(_ssd_kernel, q=q, tl=tl),
        grid=grid,
        in_specs=[tok(CONV_DIM), tok(M_INNER), tok(DT_PAD), per_b3, per_b4,
                  _const_spec((CONV_W, CONV_DIM)), _const_spec((1, CONV_DIM)), _const_spec((1, DT_PAD)),
                  _const_spec((1, DT_PAD)), _const_spec((1, M_INNER)), _const_spec((1, M_INNER)),
                  _const_spec((DT_PAD, M_INNER))],
        out_specs=(tok(M_INNER), per_b3, per_b4),
        out_shape=(jax.ShapeDtypeStruct((bsz, L, M_INNER), BF16),
                   jax.ShapeDtypeStruct((bsz, CONV_W - 1, CONV_DIM), F32),
                   jax.ShapeDtypeStruct((bsz, PAIRS, LANES, M_STATE), F32)),
        scratch_shapes=[pltpu.VMEM((8 + tl, CONV_DIM), F32), pltpu.VMEM((tl, CONV_DIM), F32),
                        pltpu.VMEM((tl, DT_PAD), F32), pltpu.VMEM((tl, DT_PAD), F32),
                        pltpu.VMEM((M_STATE, M_INNER), F32)],
        compiler_params=pltpu.CompilerParams(
            dimension_semantics=("parallel", "arbitrary"), vmem_limit_bytes=VMEM_LIMIT),
        name="ssd",
    )(xbc, z, dt, hist, h0p, cw, cb, dtb, alog, dsk, mg, ex)
    return y, nconv, nssm.reshape(bsz, M_HEADS, M_HEAD_DIM, M_STATE)


def _sb_block(kblk, vblk, qh, tri, carry, mask):
    zt = _dot_nt(kblk, qh)
    lk = -(jnp.maximum(zt, 0.0) + jnp.log(1.0 + jnp.exp(-jnp.abs(zt))))
    if mask is not None:
        lk = jnp.where(mask, lk, 0.0)
    hi, lo = _split2(lk)
    cs = _dot(tri, hi) + _dot(tri, lo)
    w = jnp.exp(zt + cs + carry)
    if mask is not None:
        w = jnp.where(mask, w, 0.0)
    return carry + cs[0:1, :], _dot_tn(vblk, w.astype(BF16))


def _suffix_tri(n):
    r = lax.broadcasted_iota(jnp.int32, (n, n), 0)
    c = lax.broadcasted_iota(jnp.int32, (n, n), 1)
    return (c >= r).astype(BF16), r < c


def _eye(n):
    r = lax.broadcasted_iota(jnp.int32, (n, n), 0)
    c = lax.broadcasted_iota(jnp.int32, (n, n), 1)
    return (r == c).astype(BF16)


def _sb_prompt_kernel(q_ref, k_ref, v_ref, o_ref, acc_s, *, tb):
    qi = pl.program_id(2)
    q2 = q_ref[0]
    lane = lax.broadcasted_iota(jnp.int32, (tb, LANES), 1)
    qh = (jnp.where(lane < SB_HEAD_DIM, q2, jnp.zeros_like(q2)),
          jnp.where(lane >= SB_HEAD_DIM, q2, jnp.zeros_like(q2)))
    tri, diag_mask = _suffix_tri(tb)
    zero = jnp.zeros((1, tb), F32)

    def step(j, carries, mask):
        rows = pl.ds(pl.multiple_of(j * tb, tb), tb)
        kblk = k_ref[0, rows, :]
        vblk = v_ref[0, rows, :]
        out = []
        for h in range(2):
            c, pv = _sb_block(kblk, vblk, qh[h], tri, carries[h], mask)
            if mask is None:
                acc_s[h] += pv
            else:
                acc_s[h] = pv
            out.append(c)
        return out

    c0, c1 = step(qi, (zero, zero), diag_mask)

    def live(c0, c1):
        return jnp.max(jnp.maximum(c0, c1))

    def cond(st):
        j, m, _, _ = st
        return jnp.logical_and(j >= 0, m > SB_LOG_CUTOFF)

    def body(st):
        j, _, c0, c1 = st
        c0, c1 = step(j, (c0, c1), None)
        return j - 1, live(c0, c1), c0, c1

    lax.while_loop(cond, body, (qi - 1, live(c0, c1), c0, c1))

    row = lax.broadcasted_iota(jnp.int32, (LANES, tb), 0)
    acc = jnp.where(row < SB_HEAD_DIM, acc_s[0], acc_s[1]).astype(BF16)
    o_ref[0] = _dot_nt(_eye(tb), acc).astype(BF16)


def _sb_prompt(q, k, v, *, tb):
    bsz, L, _ = q.shape
    grid = (bsz, SB_HEADS // 2, L // tb)
    qspec = pl.BlockSpec((1, tb, LANES), lambda b, h, i: (b, i, h))
    kspec = pl.BlockSpec((1, L, LANES), lambda b, h, i: (b, 0, h))
    return pl.pallas_call(
        functools.partial(_sb_prompt_kernel, tb=tb),
        grid=grid,
        in_specs=[qspec, kspec, kspec],
        out_specs=qspec,
        out_shape=jax.ShapeDtypeStruct((bsz, L, SB_WIDTH), BF16),
        scratch_shapes=[pltpu.VMEM((2, LANES, tb), F32)],
        compiler_params=pltpu.CompilerParams(
            dimension_semantics=("parallel", "parallel", "arbitrary"), vmem_limit_bytes=VMEM_LIMIT),
        name="sb_prompt",
    )(q, k, v)


def _sb_sample_kernel(q_ref, k_ref, v_ref, pk_ref, pv_ref, o_ref, acc_s, *, tq, tk, npast):
    q2 = q_ref[0]
    k2 = k_ref[0]
    v2 = v_ref[0]
    tri_d, diag_mask = _suffix_tri(tq)
    tri_p, _ = _suffix_tri(tk)
    zero = jnp.zeros((1, tq), F32)
    for h in range(2):
        cols = slice(h * SB_HEAD_DIM, (h + 1) * SB_HEAD_DIM)
        qh = q2[:, cols]
        c, pv = _sb_block(k2[:, cols], v2[:, cols], qh, tri_d, zero, diag_mask)
        acc_s[h] = pv

        def cond(st):
            j, m, _ = st
            return jnp.logical_and(j >= 0, m > SB_LOG_CUTOFF)

        def body(st, h=h, qh=qh):
            j, _, c = st
            rows = pl.ds(pl.multiple_of(j * tk, tk), tk)
            c, pv = _sb_block(pk_ref[0, h, rows, :].astype(BF16), pv_ref[0, h, rows, :].astype(BF16),
                              qh, tri_p, c, None)
            acc_s[h] += pv
            return j - 1, jnp.max(c), c

        lax.while_loop(cond, body, (jnp.int32(npast - 1), jnp.max(c), c))
    acc = jnp.concatenate([acc_s[0], acc_s[1]], axis=0).astype(BF16)
    o_ref[0] = _dot_nt(_eye(tq), acc).astype(BF16)


def _sb_sample(q, k, v, past_k, past_v, *, tk):
    bsz, tq, _ = q.shape
    past = past_k.shape[2]
    grid = (bsz, SB_HEADS // 2)
    qspec = pl.BlockSpec((1, tq, LANES), lambda b, h: (b, 0, h))
    pspec = pl.BlockSpec((1, 2, past, SB_HEAD_DIM), lambda b, h: (b, h, 0, 0))
    return pl.pallas_call(
        functools.partial(_sb_sample_kernel, tq=tq, tk=tk, npast=past // tk),
        grid=grid,
        in_specs=[qspec, qspec, qspec, pspec, pspec],
        out_specs=qspec,
        out_shape=jax.ShapeDtypeStruct((bsz, tq, SB_WIDTH), BF16),
        scratch_shapes=[pltpu.VMEM((2, SB_HEAD_DIM, tq), F32)],
        compiler_params=pltpu.CompilerParams(
            dimension_semantics=("parallel", "parallel"), vmem_limit_bytes=VMEM_LIMIT),
        name="sb_sample",
    )(q, k, v, past_k, past_v)


def _post_kernel(x_ref, y_ref, o_ref, ga_ref, gb_ref, p_ref, wa_ref, wb_ref, wo_ref, gf_ref, wg_ref, wu_ref,
                 wd_ref, gp_ref, wpg_ref, wp_ref, gl_ref, out_ref, *, final_norm):
    branch_a = _dot(y_ref[...], wa_ref[...])
    branch_b = _dot(o_ref[...], wb_ref[...])
    merged = jax.nn.sigmoid(ga_ref[...]) * branch_a + jax.nn.sigmoid(gb_ref[...]) * branch_b
    x = x_ref[...] + _dot(merged.astype(BF16), wo_ref[...])
    hf = _rms(x, gf_ref[...]).astype(BF16)
    gate = _dot(hf, wg_ref[...])
    act = (gate * jax.nn.sigmoid(gate) * _dot(hf, wu_ref[...])).astype(BF16)
    x = x + _dot(act, wd_ref[...])
    g = jax.nn.sigmoid(_dot(_rms(x, gp_ref[...]).astype(BF16), wpg_ref[...]))
    x = x + g * _dot(p_ref[...].astype(BF16), wp_ref[...])
    if final_norm:
        x = _rms(x, gl_ref[...])
    out_ref[...] = x


def _post(x, y, o, ga, gb, p, wts, *, tm, final_norm):
    T = x.shape[0]

    def tok(n):
        return pl.BlockSpec((tm, n), lambda i: (i, 0))

    wspecs = [_const_spec(w.shape) for w in wts]
    return pl.pallas_call(
        functools.partial(_post_kernel, final_norm=final_norm),
        grid=(T // tm,),
        in_specs=[tok(D_MODEL), tok(M_INNER), tok(SB_WIDTH), tok(D_MODEL), tok(D_MODEL), tok(PLE_DIM)] + wspecs,
        out_specs=tok(D_MODEL),
        out_shape=jax.ShapeDtypeStruct((T, D_MODEL), F32),
        compiler_params=pltpu.CompilerParams(
            dimension_semantics=("parallel",), vmem_limit_bytes=VMEM_LIMIT),
        name="post",
    )(x, y, o, ga, gb, p, *wts)


def _pack_w_in(w_in):
    dt_cols = jnp.pad(w_in[:, OFF_DT:OFF_Q], ((0, 0), (0, DT_PAD - M_HEADS)))
    parts = [w_in[:, OFF_Z:OFF_DT], dt_cols, w_in[:, OFF_Q:OFF_K] * SB_SCALE, w_in[:, OFF_K:IN_COLS]]
    return jnp.concatenate(parts, axis=1).astype(BF16)


def _head_expand():
    r = lax.broadcasted_iota(jnp.int32, (DT_PAD, M_INNER), 0)
    c = lax.broadcasted_iota(jnp.int32, (DT_PAD, M_INNER), 1)
    return (c // M_HEAD_DIM == r).astype(BF16)


def _pad_heads(v):
    return jnp.pad(v.reshape(1, M_HEADS).astype(F32), ((0, 0), (0, DT_PAD - M_HEADS)))


def _layer(x, p, past_k, past_v, conv_hist, h0, prm, *, final_norm):
    bsz, L, _ = x.shape
    prompt = past_k is None
    if prompt:
        nb, tl = 1, min(L, 256)
        conv_hist = jnp.zeros((bsz, CONV_W - 1, CONV_DIM), F32)
        h0 = jnp.zeros((bsz, M_HEADS, M_HEAD_DIM, M_STATE), F32)
    else:
        nb, tl = bsz, L
    z, xbc, dt, q, k, v, new_k, new_v, ga, gb = _inproj(x, prm["norm_mix_g"], prm["w_in"], nb=nb, tl=tl)

    chunk = L if L <= CHUNK else CHUNK
    y, new_conv, new_ssm = _ssd(xbc, z, dt, conv_hist.astype(F32), h0.astype(F32), prm["conv_w"], prm["conv_b"],
                                prm["dt_bias"], prm["a_log"], prm["d_skip"], prm["mnorm_g"], prm["head_expand"],
                                q=chunk, tl=min(L, 256))
    if prompt:
        o = _sb_prompt(q, k, v, tb=min(L, 256))
    else:
        o = _sb_sample(q, k, v, past_k.astype(F32), past_v.astype(F32), tk=min(past_k.shape[2], 256))

    T = bsz * L
    flat = lambda a: a.reshape(T, a.shape[-1])
    x = _post(flat(x), flat(y), flat(o), flat(ga), flat(gb), flat(p), prm["post"], tm=min(T, 256),
              final_norm=final_norm)
    return x.reshape(bsz, L, D_MODEL), new_k, new_v, new_conv, new_ssm


def _layer_params(i, norm_mix_g, w_in, conv_w, conv_b, dt_bias, a_log, d_skip, mnorm_g, w_a, w_b, w_out,
                  norm_ffn_g, w_gate, w_up, w_down, norm_ple_g, w_ple_gate, w_ple, final_norm_g):
    row = lambda v: v.reshape(1, -1).astype(F32)
    bf = lambda w: w.astype(BF16)
    return {
        "norm_mix_g": row(norm_mix_g[i]),
        "w_in": _pack_w_in(w_in[i]),
        "conv_w": conv_w[i].astype(F32),
        "conv_b": row(conv_b[i]),
        "dt_bias": _pad_heads(dt_bias[i]),
        "a_log": _pad_heads(a_log[i]),
        "d_skip": row(jnp.repeat(d_skip[i], M_HEAD_DIM)),
        "mnorm_g": row(mnorm_g[i]),
        "head_expand": _head_expand(),
        "post": (bf(w_a[i]), bf(w_b[i]), bf(w_out[i]), row(norm_ffn_g[i]), bf(w_gate[i]), bf(w_up[i]),
                 bf(w_down[i]), row(norm_ple_g[i]), bf(w_ple_gate[i]), bf(w_ple[i]), row(final_norm_g)),
    }


def kernel(x_prompt, x_sample, cache_k, cache_v, state_conv, state_ssm, p_prompt, p_sample, norm_mix_g, w_in,
           conv_w, conv_b, dt_bias, a_log, d_skip, mnorm_g, w_a, w_b, w_out, norm_ffn_g, w_gate, w_up, w_down,
           norm_ple_g, w_ple_gate, w_ple, final_norm_g):
    depth = w_in.shape[0]
    xp, xs = x_prompt, x_sample
    outs_p, outs_s = [], []
    for i in range(depth):
        prm = _layer_params(i, norm_mix_g, w_in, conv_w, conv_b, dt_bias, a_log, d_skip, mnorm_g, w_a, w_b,
                            w_out, norm_ffn_g, w_gate, w_up, w_down, norm_ple_g, w_ple_gate, w_ple, final_norm_g)
        last = i == depth - 1
        xp, *st_p = _layer(xp, p_prompt[i], None, None, None, None, prm, final_norm=last)
        xs, *st_s = _layer(xs, p_sample[i], cache_k[i], cache_v[i], state_conv[i], state_ssm[i], prm,
                           final_norm=last)
        outs_p.append(st_p)
        outs_s.append(st_s)
    stack = lambda outs, j: jnp.stack([o[j] for o in outs])
    return (xp, xs,
            stack(outs_p, 0), stack(outs_p, 1), stack(outs_p, 2), stack(outs_p, 3),
            stack(outs_s, 0), stack(outs_s, 1), stack(outs_s, 2), stack(outs_s, 3))
```

```python
import functools

import jax
import jax.numpy as jnp
from jax import lax
from jax.experimental import pallas as pl
from jax.experimental.pallas import tpu as pltpu

F32 = jnp.float32
BF16 = jnp.bfloat16

EPS = 1e-6
D_MODEL = 1024
CHUNK = 64
M_HEAD_DIM = 64
M_HEADS = 16
M_GROUPS = 4
M_STATE = 128
M_INNER = M_HEADS * M_HEAD_DIM
CONV_W = 4
CONV_DIM = M_INNER + 2 * M_GROUPS * M_STATE
SB_HEADS = 16
SB_HEAD_DIM = 64
SB_WIDTH = SB_HEADS * SB_HEAD_DIM
SB_SCALE = SB_HEAD_DIM ** -0.5
D_FF = 2816
PLE_DIM = 256
OFF_Z = 0
OFF_XBC = OFF_Z + M_INNER
OFF_DT = OFF_XBC + CONV_DIM
OFF_Q = OFF_DT + M_HEADS
OFF_K = OFF_Q + SB_WIDTH
OFF_V = OFF_K + SB_WIDTH
OFF_GA = OFF_V + SB_WIDTH
OFF_GB = OFF_GA + D_MODEL
IN_COLS = OFF_GB + D_MODEL

LANES = 128
DT_PAD = LANES
P_Z = 0
P_XBC = P_Z + M_INNER
P_DT = P_XBC + CONV_DIM
P_Q = P_DT + DT_PAD
P_K = P_Q + SB_WIDTH
P_V = P_K + SB_WIDTH
P_GA = P_V + SB_WIDTH
P_GB = P_GA + D_MODEL
P_COLS = P_GB + D_MODEL

VMEM_LIMIT = 56 * 1024 * 1024
SB_LOG_CUTOFF = -104.0


def _const_spec(shape):
    nd = len(shape)
    return pl.BlockSpec(shape, lambda *_: (0,) * nd, pipeline_mode=pl.Buffered(1))


def _rms(x, g):
    ms = jnp.mean(x * x, axis=-1, keepdims=True)
    return x * lax.rsqrt(ms + EPS) * g


def _split2(x):
    hi = x.astype(BF16)
    lo = (x - hi.astype(F32)).astype(BF16)
    return hi, lo


def _split3(x):
    hi = x.astype(BF16)
    r = x - hi.astype(F32)
    mid = r.astype(BF16)
    lo = (r - mid.astype(F32)).astype(BF16)
    return hi, mid, lo


def _dot(a, b):
    return jnp.dot(a, b, preferred_element_type=F32)


def _dot_nt(a, b):
    return lax.dot_general(a, b, (((1,), (1,)), ((), ())), preferred_element_type=F32)


def _dot_tn(a, b):
    return lax.dot_general(a, b, (((0,), (0,)), ((), ())), preferred_element_type=F32)


def _inproj_kernel(x_ref, g_ref, w_ref, z_ref, xbc_ref, dt_ref, q_ref, k_ref, v_ref, ko_ref, vo_ref,
                   ga_ref, gb_ref, *, nb, tl):
    tm = nb * tl
    x = x_ref[...].reshape(tm, D_MODEL)
    h = _rms(x, g_ref[...]).astype(BF16)

    def proj(lo, n):
        return _dot(h, w_ref[:, lo:lo + n])

    z_ref[...] = proj(P_Z, M_INNER).reshape(nb, tl, M_INNER)
    xbc_ref[...] = proj(P_XBC, CONV_DIM).reshape(nb, tl, CONV_DIM)
    dt_ref[...] = proj(P_DT, DT_PAD).reshape(nb, tl, DT_PAD)
    q_ref[...] = proj(P_Q, SB_WIDTH).astype(BF16).reshape(nb, tl, SB_WIDTH)
    for src, bf_ref, out_ref in ((P_K, k_ref, ko_ref), (P_V, v_ref, vo_ref)):
        kv = proj(src, SB_WIDTH)
        bf_ref[...] = kv.astype(BF16).reshape(nb, tl, SB_WIDTH)
        for b in range(nb):
            for hd in range(SB_HEADS):
                out_ref[b, hd] = kv[b * tl:(b + 1) * tl, hd * SB_HEAD_DIM:(hd + 1) * SB_HEAD_DIM]
    ga_ref[...] = proj(P_GA, D_MODEL).reshape(nb, tl, D_MODEL)
    gb_ref[...] = proj(P_GB, D_MODEL).reshape(nb, tl, D_MODEL)


def _inproj(x, g, w, *, nb, tl):
    bsz, L, _ = x.shape
    grid = (bsz // nb, L // tl)

    def tok(n):
        return pl.BlockSpec((nb, tl, n), lambda b, t: (b, t, 0))

    head = pl.BlockSpec((nb, SB_HEADS, tl, SB_HEAD_DIM), lambda b, t: (b, 0, t, 0))
    out_shape = (
        jax.ShapeDtypeStruct((bsz, L, M_INNER), F32),
        jax.ShapeDtypeStruct((bsz, L, CONV_DIM), F32),
        jax.ShapeDtypeStruct((bsz, L, DT_PAD), F32),
        jax.ShapeDtypeStruct((bsz, L, SB_WIDTH), BF16),
        jax.ShapeDtypeStruct((bsz, L, SB_WIDTH), BF16),
        jax.ShapeDtypeStruct((bsz, L, SB_WIDTH), BF16),
        jax.ShapeDtypeStruct((bsz, SB_HEADS, L, SB_HEAD_DIM), F32),
        jax.ShapeDtypeStruct((bsz, SB_HEADS, L, SB_HEAD_DIM), F32),
        jax.ShapeDtypeStruct((bsz, L, D_MODEL), F32),
        jax.ShapeDtypeStruct((bsz, L, D_MODEL), F32),
    )
    out_specs = (tok(M_INNER), tok(CONV_DIM), tok(DT_PAD), tok(SB_WIDTH), tok(SB_WIDTH), tok(SB_WIDTH),
                 head, head, tok(D_MODEL), tok(D_MODEL))
    return pl.pallas_call(
        functools.partial(_inproj_kernel, nb=nb, tl=tl),
        grid=grid,
        in_specs=[tok(D_MODEL), _const_spec((1, D_MODEL)), _const_spec((D_MODEL, P_COLS))],
        out_specs=out_specs,
        out_shape=out_shape,
        compiler_params=pltpu.CompilerParams(
            dimension_semantics=("parallel", "parallel"), vmem_limit_bytes=VMEM_LIMIT),
        name="inproj",
    )(x, g, w)


HIST_ROW = 8 - (CONV_W - 1)
PAIRS = M_HEADS // 2


def _ssd_kernel(xbc_ref, z_ref, dt_ref, hist_ref, h0_ref, cw_ref, cb_ref, dtb_ref, alog_ref, dsk_ref, mg_ref,
                ex_ref, y_ref, nconv_ref, nssm_ref, xp_s, xc_s, dt_s, da_s, h_s, *, q, tl):
    t = pl.program_id(1)
    nt = pl.num_programs(1)
    nchunk = tl // q

    @pl.when(t == 0)
    def _():
        xp_s[0:8, :] = jnp.zeros((8, CONV_DIM), F32)
        xp_s[HIST_ROW:8, :] = hist_ref[0]
        for i in range(PAIRS):
            h_s[:, i * LANES:(i + 1) * LANES] = h0_ref[0, i].T

    xp_s[8:8 + tl, :] = xbc_ref[0]
    xc = cb_ref[...] + xp_s[HIST_ROW:HIST_ROW + tl, :] * cw_ref[0:1, :]
    for w in range(1, CONV_W):
        xc = xc + xp_s[HIST_ROW + w:HIST_ROW + w + tl, :] * cw_ref[w:w + 1, :]
    xc_s[...] = xc * jax.nn.sigmoid(xc)
    tail = xp_s[8 + tl - (CONV_W - 1):8 + tl, :]

    @pl.when(t == nt - 1)
    def _():
        nconv_ref[0] = tail

    xp_s[HIST_ROW:8, :] = tail

    xdt = dt_ref[0] + dtb_ref[...]
    dt = jnp.maximum(xdt, 0.0) + jnp.log1p(jnp.exp(-jnp.abs(xdt)))
    dt_s[...] = dt
    da_s[...] = dt * (-jnp.exp(alog_ref[...]))

    ri = lax.broadcasted_iota(jnp.int32, (q, q), 0)
    ci = lax.broadcasted_iota(jnp.int32, (q, q), 1)
    causal = ri >= ci
    tri = causal.astype(BF16)
    lane = lax.broadcasted_iota(jnp.int32, (q, LANES), 1)
    even_head = lane < M_HEAD_DIM
    ex = ex_ref[...]
    pad_rows = LANES - q

    def transpose_cols(a):
        if pad_rows:
            a = jnp.concatenate([a, jnp.zeros((pad_rows, LANES), F32)], axis=0)
        return a.T[:, :q]

    def chunk(c, carry):
        r0 = pl.multiple_of(c * q, q)
        rows = pl.ds(r0, q)
        xs = xc_s[rows, 0:M_INNER]
        bm = xc_s[rows, M_INNER:M_INNER + M_GROUPS * M_STATE].astype(BF16)
        cm = xc_s[rows, M_INNER + M_GROUPS * M_STATE:CONV_DIM].astype(BF16)
        dtc = dt_s[rows, :]
        dac = da_s[rows, :]

        acum = sum(_dot(tri, p) for p in _split3(dac))
        tot = acum[q - 1:q, :]
        ey = jnp.exp(acum)
        ws = jnp.exp(tot - acum) * dtc
        dc = jnp.broadcast_to(jnp.exp(tot), (8, DT_PAD))
        stack = jnp.concatenate([ey, ws, dc], axis=0)
        stack_c = sum(_dot(p, ex) for p in _split2(stack))
        ey_c = stack_c[0:q]
        ws_c = stack_c[q:2 * q]
        dc_c = stack_c[2 * q:2 * q + 1]

        acum_t = transpose_cols(acum)
        dt_t = transpose_cols(dtc)

        xs_bf = xs.astype(BF16)
        xw_bf = (xs * ws_c).astype(BF16)
        h_prev = h_s[...]
        h_bf = h_prev.astype(BF16)

        ydiag = []
        yoff = []
        states = []
        for g in range(M_GROUPS):
            bg = bm[:, g * M_STATE:(g + 1) * M_STATE]
            cg = cm[:, g * M_STATE:(g + 1) * M_STATE]
            gcols = slice(g * 4 * M_HEAD_DIM, (g + 1) * 4 * M_HEAD_DIM)
            cb = _dot_nt(cg, bg)
            yoff.append(_dot(cg, h_bf[:, gcols]))
            states.append(_dot_tn(bg, xw_bf[:, gcols]))
            for pr in range(2):
                pair = []
                for r in range(2):
                    hd = g * 4 + pr * 2 + r
                    acol = jnp.broadcast_to(acum[:, hd:hd + 1], (q, q))
                    arow = jnp.broadcast_to(acum_t[hd:hd + 1, :], (q, q))
                    dec = jnp.exp(jnp.where(causal, acol - arow, -1e30))
                    m = (cb * dec * jnp.broadcast_to(dt_t[hd:hd + 1, :], (q, q))).astype(BF16)
                    pcols = slice((g * 2 + pr) * LANES, (g * 2 + pr + 1) * LANES)
                    pair.append(_dot(m, xs_bf[:, pcols]))
                ydiag.append(jnp.where(even_head, pair[0], pair[1]))
        y = jnp.concatenate(ydiag, axis=1) + jnp.concatenate(yoff, axis=1) * ey_c + dsk_ref[...] * xs
        zc = z_ref[0, rows, :]
        y = y * (zc * jax.nn.sigmoid(zc))
        gw = M_INNER // M_GROUPS
        normed = []
        for g in range(M_GROUPS):
            yg = y[:, g * gw:(g + 1) * gw]
            normed.append(yg * lax.rsqrt(jnp.mean(yg * yg, axis=-1, keepdims=True) + EPS))
        y = jnp.concatenate(normed, axis=1) * mg_ref[...]
        y_ref[0, rows, :] = y.astype(BF16)
        h_s[...] = h_prev * dc_c + jnp.concatenate(states, axis=1)
        return carry

    lax.fori_loop(0, nchunk, chunk, 0)

    @pl.when(t == nt - 1)
    def _():
        for i in range(PAIRS):
            nssm_ref[0, i] = h_s[:, i * LANES:(i + 1) * LANES].T


def _ssd(xbc, z, dt, hist, h0, cw, cb, dtb, alog, dsk, mg, ex, *, q, tl):
    bsz, L, _ = xbc.shape
    grid = (bsz, L // tl)
    h0p = h0.reshape(bsz, PAIRS, LANES, M_STATE)

    def tok(n):
        return pl.BlockSpec((1, tl, n), lambda b, t: (b, t, 0))

    per_b3 = pl.BlockSpec((1, CONV_W - 1, CONV_DIM), lambda b, t: (b, 0, 0))
    per_b4 = pl.BlockSpec((1, PAIRS, LANES, M_STATE), lambda b, t: (b, 0, 0, 0))
    y, nconv, nssm = pl.pallas_call(
        functools.partial(_ssd_kernel, q=q, tl=tl),
        grid=grid,
        in_specs=[tok(CONV_DIM), tok(M_INNER), tok(DT_PAD), per_b3, per_b4,
                  _const_spec((CONV_W, CONV_DIM)), _const_spec((1, CONV_DIM)), _const_spec((1, DT_PAD)),
                  _const_spec((1, DT_PAD)), _const_spec((1, M_INNER)), _const_spec((1, M_INNER)),
                  _const_spec((DT_PAD, M_INNER))],
        out_specs=(tok(M_INNER), per_b3, per_b4),
        out_shape=(jax.ShapeDtypeStruct((bsz, L, M_INNER), BF16),
                   jax.ShapeDtypeStruct((bsz, CONV_W - 1, CONV_DIM), F32),
                   jax.ShapeDtypeStruct((bsz, PAIRS, LANES, M_STATE), F32)),
        scratch_shapes=[pltpu.VMEM((8 + tl, CONV_DIM), F32), pltpu.VMEM((tl, CONV_DIM), F32),
                        pltpu.VMEM((tl, DT_PAD), F32), pltpu.VMEM((tl, DT_PAD), F32),
                        pltpu.VMEM((M_STATE, M_INNER), F32)],
        compiler_params=pltpu.CompilerParams(
            dimension_semantics=("parallel", "arbitrary"), vmem_limit_bytes=VMEM_LIMIT),
        name="ssd",
    )(xbc, z, dt, hist, h0p, cw, cb, dtb, alog, dsk, mg, ex)
    return y, nconv, nssm.reshape(bsz, M_HEADS, M_HEAD_DIM, M_STATE)


def _sb_weights(qh, kblk, tri, carry, mask):
    z = _dot_nt(qh, kblk)
    lk = -(jnp.maximum(z, 0.0) + jnp.log(1.0 + jnp.exp(-jnp.abs(z))))
    if mask is not None:
        lk = jnp.where(mask, lk, 0.0)
    hi, lo = _split2(lk)
    cs = _dot(hi, tri) + _dot(lo, tri)
    w = jnp.exp(z + cs + carry)
    if mask is not None:
        w = jnp.where(mask, w, 0.0)
    return w.astype(BF16), carry + cs[:, 0:1]


def _suffix_tri(n):
    r = lax.broadcasted_iota(jnp.int32, (n, n), 0)
    c = lax.broadcasted_iota(jnp.int32, (n, n), 1)
    return (r >= c).astype(BF16), c < r


def _split_heads(x2, first):
    zero = jnp.zeros_like(x2)
    return jnp.where(first, x2, zero), jnp.where(first, zero, x2)


def _sb_prompt_kernel(q_ref, k_ref, v_ref, o_ref, acc_s, c_s, *, tb):
    qi = pl.program_id(2)
    first = lax.broadcasted_iota(jnp.int32, (tb, LANES), 1) < SB_HEAD_DIM
    qh = _split_heads(q_ref[0], first)
    tri, diag_mask = _suffix_tri(tb)

    def step(j, mask):
        rows = pl.ds(pl.multiple_of(j * tb, tb), tb)
        k2 = k_ref[0, rows, :]
        ws = []
        live = None
        for h in range(2):
            carry = jnp.zeros((tb, 1), F32) if mask is not None else c_s[h]
            w, carry = _sb_weights(qh[h], k2, tri, carry, mask)
            c_s[h] = carry
            ws.append(w)
            m = jnp.max(carry)
            live = m if live is None else jnp.maximum(live, m)
        pv = _dot(jnp.concatenate(ws, axis=1), jnp.concatenate(_split_heads(v_ref[0, rows, :], first), axis=0))
        if mask is not None:
            acc_s[...] = pv
        else:
            acc_s[...] += pv
        return live

    def cond(st):
        j, live = st
        return jnp.logical_and(j >= 0, live > SB_LOG_CUTOFF)

    def body(st):
        j, _ = st
        return j - 1, step(j, None)

    lax.while_loop(cond, body, (qi - 1, step(qi, diag_mask)))
    o_ref[0] = acc_s[...].astype(BF16)


def _sb_prompt(q, k, v, *, tb):
    bsz, L, _ = q.shape
    grid = (bsz, SB_HEADS // 2, L // tb)
    qspec = pl.BlockSpec((1, tb, LANES), lambda b, h, i: (b, i, h))
    kspec = pl.BlockSpec((1, L, LANES), lambda b, h, i: (b, 0, h))
    return pl.pallas_call(
        functools.partial(_sb_prompt_kernel, tb=tb),
        grid=grid,
        in_specs=[qspec, kspec, kspec],
        out_specs=qspec,
        out_shape=jax.ShapeDtypeStruct((bsz, L, SB_WIDTH), BF16),
        scratch_shapes=[pltpu.VMEM((tb, LANES), F32), pltpu.VMEM((2, tb, 1), F32)],
        compiler_params=pltpu.CompilerParams(
            dimension_semantics=("parallel", "parallel", "arbitrary"), vmem_limit_bytes=VMEM_LIMIT),
        name="sb_prompt",
    )(q, k, v)


def _sb_sample_kernel(q_ref, k_ref, v_ref, pk_ref, pv_ref, o_ref, acc_s, *, tq, tk, npast):
    q2 = q_ref[0]
    first = lax.broadcasted_iota(jnp.int32, (tq, LANES), 1) < SB_HEAD_DIM
    qh = _split_heads(q2, first)
    tri_d, diag_mask = _suffix_tri(tq)
    tri_p, _ = _suffix_tri(tk)
    ws, carries = [], []
    for h in range(2):
        w, c = _sb_weights(qh[h], k_ref[0], tri_d, jnp.zeros((tq, 1), F32), diag_mask)
        ws.append(w)
        carries.append(c)
    acc_s[...] = _dot(jnp.concatenate(ws, axis=1), jnp.concatenate(_split_heads(v_ref[0], first), axis=0))

    pr = lax.broadcasted_iota(jnp.int32, (SB_HEAD_DIM, LANES), 0)
    pc = lax.broadcasted_iota(jnp.int32, (SB_HEAD_DIM, LANES), 1)
    for h in range(2):
        place = (pc == pr + h * SB_HEAD_DIM).astype(BF16)
        qs = q2[:, h * SB_HEAD_DIM:(h + 1) * SB_HEAD_DIM]

        def cond(st):
            j, live, _ = st
            return jnp.logical_and(j >= 0, live > SB_LOG_CUTOFF)

        def body(st, h=h, qs=qs, place=place):
            j, _, c = st
            rows = pl.ds(pl.multiple_of(j * tk, tk), tk)
            w, c = _sb_weights(qs, pk_ref[0, h, rows, :].astype(BF16), tri_p, c, None)
            v128 = _dot(pv_ref[0, h, rows, :].astype(BF16), place).astype(BF16)
            acc_s[...] += _dot(w, v128)
            return j - 1, jnp.max(c), c

        lax.while_loop(cond, body, (jnp.int32(npast - 1), jnp.max(carries[h]), carries[h]))
    o_ref[0] = acc_s[...].astype(BF16)


def _sb_sample(q, k, v, past_k, past_v, *, tk):
    bsz, tq, _ = q.shape
    past = past_k.shape[2]
    grid = (bsz, SB_HEADS // 2)
    qspec = pl.BlockSpec((1, tq, LANES), lambda b, h: (b, 0, h))
    pspec = pl.BlockSpec((1, 2, past, SB_HEAD_DIM), lambda b, h: (b, h, 0, 0))
    return pl.pallas_call(
        functools.partial(_sb_sample_kernel, tq=tq, tk=tk, npast=past // tk),
        grid=grid,
        in_specs=[qspec, qspec, qspec, pspec, pspec],
        out_specs=qspec,
        out_shape=jax.ShapeDtypeStruct((bsz, tq, SB_WIDTH), BF16),
        scratch_shapes=[pltpu.VMEM((tq, LANES), F32)],
        compiler_params=pltpu.CompilerParams(
            dimension_semantics=("parallel", "parallel"), vmem_limit_bytes=VMEM_LIMIT),
        name="sb_sample",
    )(q, k, v, past_k, past_v)


def _post_kernel(x_ref, y_ref, o_ref, ga_ref, gb_ref, p_ref, wa_ref, wb_ref, wo_ref, gf_ref, wg_ref, wu_ref,
                 wd_ref, gp_ref, wpg_ref, wp_ref, gl_ref, out_ref, *, final_norm):
    branch_a = _dot(y_ref[...], wa_ref[...])
    branch_b = _dot(o_ref[...], wb_ref[...])
    merged = jax.nn.sigmoid(ga_ref[...]) * branch_a + jax.nn.sigmoid(gb_ref[...]) * branch_b
    x = x_ref[...] + _dot(merged.astype(BF16), wo_ref[...])
    hf = _rms(x, gf_ref[...]).astype(BF16)
    gate = _dot(hf, wg_ref[...])
    act = (gate * jax.nn.sigmoid(gate) * _dot(hf, wu_ref[...])).astype(BF16)
    x = x + _dot(act, wd_ref[...])
    g = jax.nn.sigmoid(_dot(_rms(x, gp_ref[...]).astype(BF16), wpg_ref[...]))
    x = x + g * _dot(p_ref[...].astype(BF16), wp_ref[...])
    if final_norm:
        x = _rms(x, gl_ref[...])
    out_ref[...] = x


def _post(x, y, o, ga, gb, p, wts, *, tm, final_norm):
    T = x.shape[0]

    def tok(n):
        return pl.BlockSpec((tm, n), lambda i: (i, 0))

    wspecs = [_const_spec(w.shape) for w in wts]
    return pl.pallas_call(
        functools.partial(_post_kernel, final_norm=final_norm),
        grid=(T // tm,),
        in_specs=[tok(D_MODEL), tok(M_INNER), tok(SB_WIDTH), tok(D_MODEL), tok(D_MODEL), tok(PLE_DIM)] + wspecs,
        out_specs=tok(D_MODEL),
        out_shape=jax.ShapeDtypeStruct((T, D_MODEL), F32),
        compiler_params=pltpu.CompilerParams(
            dimension_semantics=("parallel",), vmem_limit_bytes=VMEM_LIMIT),
        name="post",
    )(x, y, o, ga, gb, p, *wts)


def _pack_w_in(w_in):
    dt_cols = jnp.pad(w_in[:, OFF_DT:OFF_Q], ((0, 0), (0, DT_PAD - M_HEADS)))
    parts = [w_in[:, OFF_Z:OFF_DT], dt_cols, w_in[:, OFF_Q:OFF_K] * SB_SCALE, w_in[:, OFF_K:IN_COLS]]
    return jnp.concatenate(parts, axis=1).astype(BF16)


def _head_expand():
    r = lax.broadcasted_iota(jnp.int32, (DT_PAD, M_INNER), 0)
    c = lax.broadcasted_iota(jnp.int32, (DT_PAD, M_INNER), 1)
    return (c // M_HEAD_DIM == r).astype(BF16)


def _pad_heads(v):
    return jnp.pad(v.reshape(1, M_HEADS).astype(F32), ((0, 0), (0, DT_PAD - M_HEADS)))


def _layer(x, p, past_k, past_v, conv_hist, h0, prm, *, final_norm):
    bsz, L, _ = x.shape
    prompt = past_k is None
    if prompt:
        nb, tl = 1, min(L, 256)
        conv_hist = jnp.zeros((bsz, CONV_W - 1, CONV_DIM), F32)
        h0 = jnp.zeros((bsz, M_HEADS, M_HEAD_DIM, M_STATE), F32)
    else:
        nb, tl = bsz, L
    z, xbc, dt, q, k, v, new_k, new_v, ga, gb = _inproj(x, prm["norm_mix_g"], prm["w_in"], nb=nb, tl=tl)

    chunk = L if L <= CHUNK else CHUNK
    y, new_conv, new_ssm = _ssd(xbc, z, dt, conv_hist.astype(F32), h0.astype(F32), prm["conv_w"], prm["conv_b"],
                                prm["dt_bias"], prm["a_log"], prm["d_skip"], prm["mnorm_g"], prm["head_expand"],
                                q=chunk, tl=min(L, 256))
    if prompt:
        o = _sb_prompt(q, k, v, tb=min(L, 256))
    else:
        o = _sb_sample(q, k, v, past_k.astype(F32), past_v.astype(F32), tk=min(past_k.shape[2], 256))

    T = bsz * L
    flat = lambda a: a.reshape(T, a.shape[-1])
    x = _post(flat(x), flat(y), flat(o), flat(ga), flat(gb), flat(p), prm["post"], tm=min(T, 256),
              final_norm=final_norm)
    return x.reshape(bsz, L, D_MODEL), new_k, new_v, new_conv, new_ssm


def _layer_params(i, norm_mix_g, w_in, conv_w, conv_b, dt_bias, a_log, d_skip, mnorm_g, w_a, w_b, w_out,
                  norm_ffn_g, w_gate, w_up, w_down, norm_ple_g, w_ple_gate, w_ple, final_norm_g):
    row = lambda v: v.reshape(1, -1).astype(F32)
    bf = lambda w: w.astype(BF16)
    return {
        "norm_mix_g": row(norm_mix_g[i]),
        "w_in": _pack_w_in(w_in[i]),
        "conv_w": conv_w[i].astype(F32),
        "conv_b": row(conv_b[i]),
        "dt_bias": _pad_heads(dt_bias[i]),
        "a_log": _pad_heads(a_log[i]),
        "d_skip": row(jnp.repeat(d_skip[i], M_HEAD_DIM)),
        "mnorm_g": row(mnorm_g[i]),
        "head_expand": _head_expand(),
        "post": (bf(w_a[i]), bf(w_b[i]), bf(w_out[i]), row(norm_ffn_g[i]), bf(w_gate[i]), bf(w_up[i]),
                 bf(w_down[i]), row(norm_ple_g[i]), bf(w_ple_gate[i]), bf(w_ple[i]), row(final_norm_g)),
    }


def kernel(x_prompt, x_sample, cache_k, cache_v, state_conv, state_ssm, p_prompt, p_sample, norm_mix_g, w_in,
           conv_w, conv_b, dt_bias, a_log, d_skip, mnorm_g, w_a, w_b, w_out, norm_ffn_g, w_gate, w_up, w_down,
           norm_ple_g, w_ple_gate, w_ple, final_norm_g):
    depth = w_in.shape[0]
    xp, xs = x_prompt, x_sample
    outs_p, outs_s = [], []
    for i in range(depth):
        prm = _layer_params(i, norm_mix_g, w_in, conv_w, conv_b, dt_bias, a_log, d_skip, mnorm_g, w_a, w_b,
                            w_out, norm_ffn_g, w_gate, w_up, w_down, norm_ple_g, w_ple_gate, w_ple, final_norm_g)
        last = i == depth - 1
        xp, *st_p = _layer(xp, p_prompt[i], None, None, None, None, prm, final_norm=last)
        xs, *st_s = _layer(xs, p_sample[i], cache_k[i], cache_v[i], state_conv[i], state_ssm[i], prm,
                           final_norm=last)
        outs_p.append(st_p)
        outs_s.append(st_s)
    stack = lambda outs, j: jnp.stack([o[j] for o in outs])
    return (xp, xs,
            stack(outs_p, 0), stack(outs_p, 1), stack(outs_p, 2), stack(outs_p, 3),
            stack(outs_s, 0), stack(outs_s, 1), stack(outs_s, 2), stack(outs_s, 3))
```

```python
import functools

import jax
import jax.numpy as jnp
from jax import lax
from jax.experimental import pallas as pl
from jax.experimental.pallas import tpu as pltpu

F32 = jnp.float32
BF16 = jnp.bfloat16

EPS = 1e-6
D_MODEL = 1024
CHUNK = 64
M_HEAD_DIM = 64
M_HEADS = 16
M_GROUPS = 4
M_STATE = 128
M_INNER = M_HEADS * M_HEAD_DIM
CONV_W = 4
CONV_DIM = M_INNER + 2 * M_GROUPS * M_STATE
SB_HEADS = 16
SB_HEAD_DIM = 64
SB_WIDTH = SB_HEADS * SB_HEAD_DIM
SB_SCALE = SB_HEAD_DIM ** -0.5
D_FF = 2816
PLE_DIM = 256
OFF_Z = 0
OFF_XBC = OFF_Z + M_INNER
OFF_DT = OFF_XBC + CONV_DIM
OFF_Q = OFF_DT + M_HEADS
OFF_K = OFF_Q + SB_WIDTH
OFF_V = OFF_K + SB_WIDTH
OFF_GA = OFF_V + SB_WIDTH
OFF_GB = OFF_GA + D_MODEL
IN_COLS = OFF_GB + D_MODEL

LANES = 128
DT_PAD = LANES
P_Z = 0
P_XBC = P_Z + M_INNER
P_DT = P_XBC + CONV_DIM
P_Q = P_DT + DT_PAD
P_K = P_Q + SB_WIDTH
P_V = P_K + SB_WIDTH
P_GA = P_V + SB_WIDTH
P_GB = P_GA + D_MODEL
P_COLS = P_GB + D_MODEL

VMEM_LIMIT = 56 * 1024 * 1024
SB_LOG_CUTOFF = -104.0


def _const_spec(shape):
    nd = len(shape)
    return pl.BlockSpec(shape, lambda *_: (0,) * nd, pipeline_mode=pl.Buffered(1))


def _rms(x, g):
    ms = jnp.mean(x * x, axis=-1, keepdims=True)
    return x * lax.rsqrt(ms + EPS) * g


def _split2(x):
    hi = x.astype(BF16)
    lo = (x - hi.astype(F32)).astype(BF16)
    return hi, lo


def _split3(x):
    hi = x.astype(BF16)
    r = x - hi.astype(F32)
    mid = r.astype(BF16)
    lo = (r - mid.astype(F32)).astype(BF16)
    return hi, mid, lo


def _dot(a, b):
    return jnp.dot(a, b, preferred_element_type=F32)


def _dot_nt(a, b):
    return lax.dot_general(a, b, (((1,), (1,)), ((), ())), preferred_element_type=F32)


def _dot_tn(a, b):
    return lax.dot_general(a, b, (((0,), (0,)), ((), ())), preferred_element_type=F32)


def _inproj_kernel(x_ref, g_ref, w_ref, z_ref, xbc_ref, dt_ref, q_ref, k_ref, v_ref, ko_ref, vo_ref,
                   ga_ref, gb_ref, *, nb, tl, kv_feature_major):
    tm = nb * tl
    x = x_ref[...].reshape(tm, D_MODEL)
    h = _rms(x, g_ref[...]).astype(BF16)

    def proj(lo, n):
        return _dot(h, w_ref[:, lo:lo + n])

    z_ref[...] = proj(P_Z, M_INNER).reshape(nb, tl, M_INNER)
    xbc_ref[...] = proj(P_XBC, CONV_DIM).reshape(nb, tl, CONV_DIM)
    dt_ref[...] = proj(P_DT, DT_PAD).reshape(nb, tl, DT_PAD)
    q_ref[...] = proj(P_Q, SB_WIDTH).astype(BF16).reshape(nb, tl, SB_WIDTH)
    for src, bf_ref, out_ref in ((P_K, k_ref, ko_ref), (P_V, v_ref, vo_ref)):
        kv = proj(src, SB_WIDTH)
        bf_ref[...] = kv.astype(BF16).reshape(nb, tl, SB_WIDTH)
        if kv_feature_major:
            out_ref[0] = kv.T.reshape(SB_HEADS, SB_HEAD_DIM, tl)
        else:
            for b in range(nb):
                for hd in range(SB_HEADS):
                    out_ref[b, hd] = kv[b * tl:(b + 1) * tl, hd * SB_HEAD_DIM:(hd + 1) * SB_HEAD_DIM]
    ga_ref[...] = proj(P_GA, D_MODEL).reshape(nb, tl, D_MODEL)
    gb_ref[...] = proj(P_GB, D_MODEL).reshape(nb, tl, D_MODEL)


def _inproj(x, g, w, *, nb, tl, kv_feature_major):
    bsz, L, _ = x.shape
    grid = (bsz // nb, L // tl)

    def tok(n):
        return pl.BlockSpec((nb, tl, n), lambda b, t: (b, t, 0))

    if kv_feature_major:
        assert nb == 1
        head_shape = (bsz, SB_HEADS, SB_HEAD_DIM, L)
        head = pl.BlockSpec((1, SB_HEADS, SB_HEAD_DIM, tl), lambda b, t: (b, 0, 0, t))
    else:
        head_shape = (bsz, SB_HEADS, L, SB_HEAD_DIM)
        head = pl.BlockSpec((nb, SB_HEADS, tl, SB_HEAD_DIM), lambda b, t: (b, 0, t, 0))
    out_shape = (
        jax.ShapeDtypeStruct((bsz, L, M_INNER), F32),
        jax.ShapeDtypeStruct((bsz, L, CONV_DIM), F32),
        jax.ShapeDtypeStruct((bsz, L, DT_PAD), F32),
        jax.ShapeDtypeStruct((bsz, L, SB_WIDTH), BF16),
        jax.ShapeDtypeStruct((bsz, L, SB_WIDTH), BF16),
        jax.ShapeDtypeStruct((bsz, L, SB_WIDTH), BF16),
        jax.ShapeDtypeStruct(head_shape, F32),
        jax.ShapeDtypeStruct(head_shape, F32),
        jax.ShapeDtypeStruct((bsz, L, D_MODEL), F32),
        jax.ShapeDtypeStruct((bsz, L, D_MODEL), F32),
    )
    out_specs = (tok(M_INNER), tok(CONV_DIM), tok(DT_PAD), tok(SB_WIDTH), tok(SB_WIDTH), tok(SB_WIDTH),
                 head, head, tok(D_MODEL), tok(D_MODEL))
    return pl.pallas_call(
        functools.partial(_inproj_kernel, nb=nb, tl=tl, kv_feature_major=kv_feature_major),
        grid=grid,
        in_specs=[tok(D_MODEL), _const_spec((1, D_MODEL)), _const_spec((D_MODEL, P_COLS))],
        out_specs=out_specs,
        out_shape=out_shape,
        compiler_params=pltpu.CompilerParams(
            dimension_semantics=("parallel", "parallel"), vmem_limit_bytes=VMEM_LIMIT),
        name="inproj",
    )(x, g, w)


HIST_ROW = 8 - (CONV_W - 1)
PAIRS = M_HEADS // 2


def _ssd_kernel(xbc_ref, z_ref, dt_ref, hist_ref, h0_ref, cw_ref, cb_ref, dtb_ref, alog_ref, dsk_ref, mg_ref,
                ex_ref, y_ref, nconv_ref, nssm_ref, xp_s, xc_s, dt_s, da_s, h_s, *, q, tl):
    t = pl.program_id(1)
    nt = pl.num_programs(1)
    nchunk = tl // q

    @pl.when(t == 0)
    def _():
        xp_s[0:8, :] = jnp.zeros((8, CONV_DIM), F32)
        xp_s[HIST_ROW:8, :] = hist_ref[0]
        for i in range(PAIRS):
            h_s[:, i * LANES:(i + 1) * LANES] = h0_ref[0, i].T

    xp_s[8:8 + tl, :] = xbc_ref[0]
    xc = cb_ref[...] + xp_s[HIST_ROW:HIST_ROW + tl, :] * cw_ref[0:1, :]
    for w in range(1, CONV_W):
        xc = xc + xp_s[HIST_ROW + w:HIST_ROW + w + tl, :] * cw_ref[w:w + 1, :]
    xc_s[...] = xc * jax.nn.sigmoid(xc)
    tail = xp_s[8 + tl - (CONV_W - 1):8 + tl, :]

    @pl.when(t == nt - 1)
    def _():
        nconv_ref[0] = tail

    xp_s[HIST_ROW:8, :] = tail

    xdt = dt_ref[0] + dtb_ref[...]
    dt = jnp.maximum(xdt, 0.0) + jnp.log1p(jnp.exp(-jnp.abs(xdt)))
    dt_s[...] = dt
    da_s[...] = dt * (-jnp.exp(alog_ref[...]))

    ri = lax.broadcasted_iota(jnp.int32, (q, q), 0)
    ci = lax.broadcasted_iota(jnp.int32, (q, q), 1)
    causal = ri >= ci
    tri = causal.astype(BF16)
    lane = lax.broadcasted_iota(jnp.int32, (q, LANES), 1)
    even_head = lane < M_HEAD_DIM
    ex = ex_ref[...]
    pad_rows = LANES - q

    def transpose_cols(a):
        if pad_rows:
            a = jnp.concatenate([a, jnp.zeros((pad_rows, LANES), F32)], axis=0)
        return a.T[:, :q]

    def chunk(c, carry):
        r0 = pl.multiple_of(c * q, q)
        rows = pl.ds(r0, q)
        xs = xc_s[rows, 0:M_INNER]
        bm = xc_s[rows, M_INNER:M_INNER + M_GROUPS * M_STATE].astype(BF16)
        cm = xc_s[rows, M_INNER + M_GROUPS * M_STATE:CONV_DIM].astype(BF16)
        dtc = dt_s[rows, :]
        dac = da_s[rows, :]

        acum = sum(_dot(tri, p) for p in _split3(dac))
        tot = acum[q - 1:q, :]
        ey = jnp.exp(acum)
        ws = jnp.exp(tot - acum) * dtc
        dc = jnp.broadcast_to(jnp.exp(tot), (8, DT_PAD))
        stack = jnp.concatenate([ey, ws, dc], axis=0)
        stack_c = sum(_dot(p, ex) for p in _split2(stack))
        ey_c = stack_c[0:q]
        ws_c = stack_c[q:2 * q]
        dc_c = stack_c[2 * q:2 * q + 1]

        acum_t = transpose_cols(acum)
        dt_t = transpose_cols(dtc)

        xs_bf = xs.astype(BF16)
        xw_bf = (xs * ws_c).astype(BF16)
        h_prev = h_s[...]
        h_bf = h_prev.astype(BF16)

        ydiag = []
        yoff = []
        states = []
        for g in range(M_GROUPS):
            bg = bm[:, g * M_STATE:(g + 1) * M_STATE]
            cg = cm[:, g * M_STATE:(g + 1) * M_STATE]
            gcols = slice(g * 4 * M_HEAD_DIM, (g + 1) * 4 * M_HEAD_DIM)
            cb = _dot_nt(cg, bg)
            yoff.append(_dot(cg, h_bf[:, gcols]))
            states.append(_dot_tn(bg, xw_bf[:, gcols]))
            for pr in range(2):
                pair = []
                for r in range(2):
                    hd = g * 4 + pr * 2 + r
                    acol = jnp.broadcast_to(acum[:, hd:hd + 1], (q, q))
                    arow = jnp.broadcast_to(acum_t[hd:hd + 1, :], (q, q))
                    dec = jnp.exp(jnp.where(causal, acol - arow, -1e30))
                    m = (cb * dec * jnp.broadcast_to(dt_t[hd:hd + 1, :], (q, q))).astype(BF16)
                    pcols = slice((g * 2 + pr) * LANES, (g * 2 + pr + 1) * LANES)
                    pair.append(_dot(m, xs_bf[:, pcols]))
                ydiag.append(jnp.where(even_head, pair[0], pair[1]))
        y = jnp.concatenate(ydiag, axis=1) + jnp.concatenate(yoff, axis=1) * ey_c + dsk_ref[...] * xs
        zc = z_ref[0, rows, :]
        y = y * (zc * jax.nn.sigmoid(zc))
        gw = M_INNER // M_GROUPS
        normed = []
        for g in range(M_GROUPS):
            yg = y[:, g * gw:(g + 1) * gw]
            normed.append(yg * lax.rsqrt(jnp.mean(yg * yg, axis=-1, keepdims=True) + EPS))
        y = jnp.concatenate(normed, axis=1) * mg_ref[...]
        y_ref[0, rows, :] = y.astype(BF16)
        h_s[...] = h_prev * dc_c + jnp.concatenate(states, axis=1)
        return carry

    lax.fori_loop(0, nchunk, chunk, 0)

    @pl.when(t == nt - 1)
    def _():
        for i in range(PAIRS):
            nssm_ref[0, i] = h_s[:, i * LANES:(i + 1) * LANES].T


def _ssd(xbc, z, dt, hist, h0, cw, cb, dtb, alog, dsk, mg, ex, *, q, tl):
    bsz, L, _ = xbc.shape
    grid = (bsz, L // tl)
    h0p = h0.reshape(bsz, PAIRS, LANES, M_STATE)

    def tok(n):
        return pl.BlockSpec((1, tl, n), lambda b, t: (b, t, 0))

    per_b3 = pl.BlockSpec((1, CONV_W - 1, CONV_DIM), lambda b, t: (b, 0, 0))
    per_b4 = pl.BlockSpec((1, PAIRS, LANES, M_STATE), lambda b, t: (b, 0, 0, 0))
    y, nconv, nssm = pl.pallas_call(
        functools.partial(_ssd_kernel, q=q, tl=tl),
        grid=grid,
        in_specs=[tok(CONV_DIM), tok(M_INNER), tok(DT_PAD), per_b3, per_b4,
                  _const_spec((CONV_W, CONV_DIM)), _const_spec((1, CONV_DIM)), _const_spec((1, DT_PAD)),
                  _const_spec((1, DT_PAD)), _const_spec((1, M_INNER)), _const_spec((1, M_INNER)),
                  _const_spec((DT_PAD, M_INNER))],
        out_specs=(tok(M_INNER), per_b3, per_b4),
        out_shape=(jax.ShapeDtypeStruct((bsz, L, M_INNER), BF16),
                   jax.ShapeDtypeStruct((bsz, CONV_W - 1, CONV_DIM), F32),
                   jax.ShapeDtypeStruct((bsz, PAIRS, LANES, M_STATE), F32)),
        scratch_shapes=[pltpu.VMEM((8 + tl, CONV_DIM), F32), pltpu.VMEM((tl, CONV_DIM), F32),
                        pltpu.VMEM((tl, DT_PAD), F32), pltpu.VMEM((tl, DT_PAD), F32),
                        pltpu.VMEM((M_STATE, M_INNER), F32)],
        compiler_params=pltpu.CompilerParams(
            dimension_semantics=("parallel", "arbitrary"), vmem_limit_bytes=VMEM_LIMIT),
        name="ssd",
    )(xbc, z, dt, hist, h0p, cw, cb, dtb, alog, dsk, mg, ex)
    return y, nconv, nssm.reshape(bsz, M_HEADS, M_HEAD_DIM, M_STATE)


def _sb_weights(z, tri, carry, mask):
    lk =-(jnp.maximum(z, 0.0) + jnp.log(1.0 + jnp.exp(-jnp.abs(z))))
    if mask is not None:
        lk = jnp.where(mask, lk, 0.0)
    hi, lo = _split2(lk)
    cs = _dot(hi, tri) + _dot(lo, tri)
    w = jnp.exp(z + cs + carry)
    if mask is not None:
        w = jnp.where(mask, w, 0.0)
    return w.astype(BF16), carry + cs[:, 0:1]


def _suffix_tri(n):
    r = lax.broadcasted_iota(jnp.int32, (n, n), 0)
    c = lax.broadcasted_iota(jnp.int32, (n, n), 1)
    return (r >= c).astype(BF16), c < r


def _split_heads(x2, first):
    zero = jnp.zeros_like(x2)
    return jnp.where(first, x2, zero), jnp.where(first, zero, x2)


def _sb_prompt_kernel(q_ref, k_ref, v_ref, o_ref, acc_s, c_s, *, tb):
    qi = pl.program_id(2)
    first = lax.broadcasted_iota(jnp.int32, (tb, LANES), 1) < SB_HEAD_DIM
    qh = _split_heads(q_ref[0], first)
    tri, diag_mask = _suffix_tri(tb)

    def step(j, mask):
        rows = pl.ds(pl.multiple_of(j * tb, tb), tb)
        k2 = k_ref[0, rows, :]
        ws = []
        live = None
        for h in range(2):
            carry = jnp.zeros((tb, 1), F32) if mask is not None else c_s[h]
            w, carry = _sb_weights(_dot_nt(qh[h], k2), tri, carry, mask)
            c_s[h] = carry
            ws.append(w)
            m = jnp.max(carry)
            live = m if live is None else jnp.maximum(live, m)
        pv = _dot(jnp.concatenate(ws, axis=1), jnp.concatenate(_split_heads(v_ref[0, rows, :], first), axis=0))
        if mask is not None:
            acc_s[...] = pv
        else:
            acc_s[...] += pv
        return live

    def cond(st):
        j, live = st
        return jnp.logical_and(j >= 0, live > SB_LOG_CUTOFF)

    def body(st):
        j, _ = st
        return j - 1, step(j, None)

    lax.while_loop(cond, body, (qi - 1, step(qi, diag_mask)))
    o_ref[0] = acc_s[...].astype(BF16)


def _sb_prompt(q, k, v, *, tb):
    bsz, L, _ = q.shape
    grid = (bsz, SB_HEADS // 2, L // tb)
    qspec = pl.BlockSpec((1, tb, LANES), lambda b, h, i: (b, i, h))
    kspec = pl.BlockSpec((1, L, LANES), lambda b, h, i: (b, 0, h))
    return pl.pallas_call(
        functools.partial(_sb_prompt_kernel, tb=tb),
        grid=grid,
        in_specs=[qspec, kspec, kspec],
        out_specs=qspec,
        out_shape=jax.ShapeDtypeStruct((bsz, L, SB_WIDTH), BF16),
        scratch_shapes=[pltpu.VMEM((tb, LANES), F32), pltpu.VMEM((2, tb, 1), F32)],
        compiler_params=pltpu.CompilerParams(
            dimension_semantics=("parallel", "parallel", "arbitrary"), vmem_limit_bytes=VMEM_LIMIT),
        name="sb_prompt",
    )(q, k, v)


def _sb_sample_kernel(q_ref, k_ref, v_ref, pk_ref, pv_ref, o_ref, acc_s, *, tq, tk, npast):
    q2 = q_ref[0]
    first = lax.broadcasted_iota(jnp.int32, (tq, LANES), 1) < SB_HEAD_DIM
    qh = _split_heads(q2, first)
    tri_d, diag_mask = _suffix_tri(tq)
    tri_p, _ = _suffix_tri(tk)
    ws, carries = [], []
    for h in range(2):
        w, c = _sb_weights(_dot_nt(qh[h], k_ref[0]), tri_d, jnp.zeros((tq, 1), F32), diag_mask)
        ws.append(w)
        carries.append(c)
    acc_s[...] = _dot(jnp.concatenate(ws, axis=1), jnp.concatenate(_split_heads(v_ref[0], first), axis=0))

    pr = lax.broadcasted_iota(jnp.int32, (LANES, SB_HEAD_DIM), 0)
    pc = lax.broadcasted_iota(jnp.int32, (LANES, SB_HEAD_DIM), 1)
    for h in range(2):
        place = (pr == pc + h * SB_HEAD_DIM).astype(BF16)
        qs = q2[:, h * SB_HEAD_DIM:(h + 1) * SB_HEAD_DIM]

        def cond(st):
            j, live, _ = st
            return jnp.logical_and(j >= 0, live > SB_LOG_CUTOFF)

        def body(st, h=h, qs=qs, place=place):
            j, _, c = st
            cols = pl.ds(pl.multiple_of(j * tk, tk), tk)
            w, c = _sb_weights(_dot(qs, pk_ref[0, h, :, cols].astype(BF16)), tri_p, c, None)
            vt = _dot(place, pv_ref[0, h, :, cols].astype(BF16)).astype(BF16)
            acc_s[...] += _dot_nt(w, vt)
            return j - 1, jnp.max(c), c

        lax.while_loop(cond, body, (jnp.int32(npast - 1), jnp.max(carries[h]), carries[h]))
    o_ref[0] = acc_s[...].astype(BF16)


def _sb_sample(q, k, v, past_kt, past_vt, *, tk):
    bsz, tq, _ = q.shape
    past = past_kt.shape[3]
    grid = (bsz, SB_HEADS // 2)
    qspec = pl.BlockSpec((1, tq, LANES), lambda b, h: (b, 0, h))
    pspec = pl.BlockSpec((1, 2, SB_HEAD_DIM, past), lambda b, h: (b, h, 0, 0))
    return pl.pallas_call(
        functools.partial(_sb_sample_kernel, tq=tq, tk=tk, npast=past // tk),
        grid=grid,
        in_specs=[qspec, qspec, qspec, pspec, pspec],
        out_specs=qspec,
        out_shape=jax.ShapeDtypeStruct((bsz, tq, SB_WIDTH), BF16),
        scratch_shapes=[pltpu.VMEM((tq, LANES), F32)],
        compiler_params=pltpu.CompilerParams(
            dimension_semantics=("parallel", "parallel"), vmem_limit_bytes=VMEM_LIMIT),
        name="sb_sample",
    )(q, k, v, past_kt, past_vt)


def _post_kernel(x_ref, y_ref, o_ref, ga_ref, gb_ref, p_ref, wa_ref, wb_ref, wo_ref, gf_ref, wg_ref, wu_ref,
                 wd_ref, gp_ref, wpg_ref, wp_ref, gl_ref, out_ref, *, final_norm):
    branch_a = _dot(y_ref[...], wa_ref[...])
    branch_b = _dot(o_ref[...], wb_ref[...])
    merged = jax.nn.sigmoid(ga_ref[...]) * branch_a + jax.nn.sigmoid(gb_ref[...]) * branch_b
    x = x_ref[...] + _dot(merged.astype(BF16), wo_ref[...])
    hf = _rms(x, gf_ref[...]).astype(BF16)
    gate = _dot(hf, wg_ref[...])
    act = (gate * jax.nn.sigmoid(gate) * _dot(hf, wu_ref[...])).astype(BF16)
    x = x + _dot(act, wd_ref[...])
    g = jax.nn.sigmoid(_dot(_rms(x, gp_ref[...]).astype(BF16), wpg_ref[...]))
    x = x + g * _dot(p_ref[...].astype(BF16), wp_ref[...])
    if final_norm:
        x = _rms(x, gl_ref[...])
    out_ref[...] = x


def _post(x, y, o, ga, gb, p, wts, *, tm, final_norm):
    T = x.shape[0]

    def tok(n):
        return pl.BlockSpec((tm, n), lambda i: (i, 0))

    wspecs = [_const_spec(w.shape) for w in wts]
    return pl.pallas_call(
        functools.partial(_post_kernel, final_norm=final_norm),
        grid=(T // tm,),
        in_specs=[tok(D_MODEL), tok(M_INNER), tok(SB_WIDTH), tok(D_MODEL), tok(D_MODEL), tok(PLE_DIM)] + wspecs,
        out_specs=tok(D_MODEL),
        out_shape=jax.ShapeDtypeStruct((T, D_MODEL), F32),
        compiler_params=pltpu.CompilerParams(
            dimension_semantics=("parallel",), vmem_limit_bytes=VMEM_LIMIT),
        name="post",
    )(x, y, o, ga, gb, p, *wts)


def _pack_w_in(w_in):
    dt_cols = jnp.pad(w_in[:, OFF_DT:OFF_Q], ((0, 0), (0, DT_PAD - M_HEADS)))
    parts = [w_in[:, OFF_Z:OFF_DT], dt_cols, w_in[:, OFF_Q:OFF_K] * SB_SCALE, w_in[:, OFF_K:IN_COLS]]
    return jnp.concatenate(parts, axis=1).astype(BF16)


def _head_expand():
    r = lax.broadcasted_iota(jnp.int32, (DT_PAD, M_INNER), 0)
    c = lax.broadcasted_iota(jnp.int32, (DT_PAD, M_INNER), 1)
    return (c // M_HEAD_DIM == r).astype(BF16)


def _pad_heads(v):
    return jnp.pad(v.reshape(1, M_HEADS).astype(F32), ((0, 0), (0, DT_PAD - M_HEADS)))


def _layer(x, p, past_k, past_v, conv_hist, h0, prm, *, final_norm):
    bsz, L, _ = x.shape
    prompt = past_k is None
    if prompt:
        nb, tl = 1, min(L, 256)
        conv_hist = jnp.zeros((bsz, CONV_W - 1, CONV_DIM), F32)
        h0 = jnp.zeros((bsz, M_HEADS, M_HEAD_DIM, M_STATE), F32)
    else:
        nb, tl = bsz, L
    z, xbc, dt, q, k, v, new_k, new_v, ga, gb = _inproj(x, prm["norm_mix_g"], prm["w_in"], nb=nb, tl=tl,
                                                        kv_feature_major=prompt)
    if prompt:
        new_k, new_v = jnp.swapaxes(new_k, 2, 3), jnp.swapaxes(new_v, 2, 3)

    chunk = L if L <= CHUNK else CHUNK
    y, new_conv, new_ssm = _ssd(xbc, z, dt, conv_hist.astype(F32), h0.astype(F32), prm["conv_w"], prm["conv_b"],
                                prm["dt_bias"], prm["a_log"], prm["d_skip"], prm["mnorm_g"], prm["head_expand"],
                                q=chunk, tl=min(L, 256))
    if prompt:
        o = _sb_prompt(q, k, v, tb=min(L, 256))
    else:
        o = _sb_sample(q, k, v, jnp.swapaxes(past_k.astype(F32), 2, 3), jnp.swapaxes(past_v.astype(F32), 2, 3),
                       tk=min(past_k.shape[2], 256))

    T = bsz * L
    flat = lambda a: a.reshape(T, a.shape[-1])
    x = _post(flat(x), flat(y), flat(o), flat(ga), flat(gb), flat(p), prm["post"], tm=min(T, 256),
              final_norm=final_norm)
    return x.reshape(bsz, L, D_MODEL), new_k, new_v, new_conv, new_ssm


def _layer_params(i, norm_mix_g, w_in, conv_w, conv_b, dt_bias, a_log, d_skip, mnorm_g, w_a, w_b, w_out,
                  norm_ffn_g, w_gate, w_up, w_down, norm_ple_g, w_ple_gate, w_ple, final_norm_g):
    row = lambda v: v.reshape(1, -1).astype(F32)
    bf = lambda w: w.astype(BF16)
    return {
        "norm_mix_g": row(norm_mix_g[i]),
        "w_in": _pack_w_in(w_in[i]),
        "conv_w": conv_w[i].astype(F32),
        "conv_b": row(conv_b[i]),
        "dt_bias": _pad_heads(dt_bias[i]),
        "a_log": _pad_heads(a_log[i]),
        "d_skip": row(jnp.repeat(d_skip[i], M_HEAD_DIM)),
        "mnorm_g": row(mnorm_g[i]),
        "head_expand": _head_expand(),
        "post": (bf(w_a[i]), bf(w_b[i]), bf(w_out[i]), row(norm_ffn_g[i]), bf(w_gate[i]), bf(w_up[i]),
                 bf(w_down[i]), row(norm_ple_g[i]), bf(w_ple_gate[i]), bf(w_ple[i]), row(final_norm_g)),
    }


def kernel(x_prompt, x_sample, cache_k, cache_v, state_conv, state_ssm, p_prompt, p_sample, norm_mix_g, w_in,
           conv_w, conv_b, dt_bias, a_log, d_skip, mnorm_g, w_a, w_b, w_out, norm_ffn_g, w_gate, w_up, w_down,
           norm_ple_g, w_ple_gate, w_ple, final_norm_g):
    depth = w_in.shape[0]
    xp, xs = x_prompt, x_sample
    outs_p, outs_s = [], []
    for i in range(depth):
        prm = _layer_params(i, norm_mix_g, w_in, conv_w, conv_b, dt_bias, a_log, d_skip, mnorm_g, w_a, w_b,
                            w_out, norm_ffn_g, w_gate, w_up, w_down, norm_ple_g, w_ple_gate, w_ple, final_norm_g)
        last = i == depth - 1
        xp, *st_p = _layer(xp, p_prompt[i], None, None, None, None, prm, final_norm=last)
        xs, *st_s = _layer(xs, p_sample[i], cache_k[i], cache_v[i], state_conv[i], state_ssm[i], prm,
                           final_norm=last)
        outs_p.append(st_p)
        outs_s.append(st_s)
    stack = lambda outs, j: jnp.stack([o[j] for o in outs])
    return (xp, xs,
            stack(outs_p, 0), stack(outs_p, 1), stack(outs_p, 2), stack(outs_p, 3),
            stack(outs_s, 0), stack(outs_s, 1), stack(outs_s, 2), stack(outs_s, 3))
```

```python
import functools

import jax
import jax.numpy as jnp
from jax import lax
from jax.experimental import pallas as pl
from jax.experimental.pallas import tpu as pltpu

F32 = jnp.float32
BF16 = jnp.bfloat16

EPS = 1e-6
D_MODEL = 1024
M_HEAD_DIM = 64
M_HEADS = 16
M_GROUPS = 4
M_STATE = 128
M_INNER = M_HEADS * M_HEAD_DIM
CONV_W = 4
CONV_DIM = M_INNER + 2 * M_GROUPS * M_STATE
SB_HEADS = 16
SB_HEAD_DIM = 64
SB_WIDTH = SB_HEADS * SB_HEAD_DIM
SB_SCALE = SB_HEAD_DIM ** -0.5
D_FF = 2816
PLE_DIM = 256
OFF_Z = 0
OFF_XBC = OFF_Z + M_INNER
OFF_DT = OFF_XBC + CONV_DIM
OFF_Q = OFF_DT + M_HEADS
OFF_K = OFF_Q + SB_WIDTH
OFF_V = OFF_K + SB_WIDTH
OFF_GA = OFF_V + SB_WIDTH
OFF_GB = OFF_GA + D_MODEL
IN_COLS = OFF_GB + D_MODEL

LANES = 128
DT_PAD = LANES
P_Z = 0
P_XBC = P_Z + M_INNER
P_DT = P_XBC + CONV_DIM
P_Q = P_DT + DT_PAD
P_K = P_Q + SB_WIDTH
P_V = P_K + SB_WIDTH
P_GA = P_V + SB_WIDTH
P_GB = P_GA + D_MODEL
P_COLS = P_GB + D_MODEL

VMEM_LIMIT = 56 * 1024 * 1024
SB_LOG_CUTOFF = -104.0
TOKEN_TILE = 256
SB_BLOCK = 256
SSD_CHUNK = 256


def _const_spec(shape):
    nd = len(shape)
    return pl.BlockSpec(shape, lambda *_: (0,) * nd, pipeline_mode=pl.Buffered(1))


def _rms(x, g):
    ms = jnp.mean(x * x, axis=-1, keepdims=True)
    return x * lax.rsqrt(ms + EPS) * g


def _split2(x):
    hi = x.astype(BF16)
    lo = (x - hi.astype(F32)).astype(BF16)
    return hi, lo


def _split3(x):
    hi = x.astype(BF16)
    r = x - hi.astype(F32)
    mid = r.astype(BF16)
    lo = (r - mid.astype(F32)).astype(BF16)
    return hi, mid, lo


def _dot(a, b):
    return jnp.dot(a, b, preferred_element_type=F32)


def _dot_nt(a, b):
    return lax.dot_general(a, b, (((1,), (1,)), ((), ())), preferred_element_type=F32)


def _dot_tn(a, b):
    return lax.dot_general(a, b, (((0,), (0,)), ((), ())), preferred_element_type=F32)


def _split_heads(x2, first):
    zero = jnp.zeros_like(x2)
    return jnp.where(first, x2, zero), jnp.where(first, zero, x2)


HIST_ROW = 8 - (CONV_W - 1)
BC_COLS = 2 * M_GROUPS * M_STATE


def _inproj_kernel(x_ref, hist_ref, g_ref, w_ref, cw_ref, cb_ref, zs_ref, xs_ref, bc_ref, dt_ref, q_ref, k_ref,
                   v_ref, ko_ref, vo_ref, ga_ref, gb_ref, nconv_ref, xp_s, *, nb, tl, kv_feature_major):
    t = pl.program_id(1)
    tm = nb * tl
    x = x_ref[...].reshape(tm, D_MODEL)
    h = _rms(x, g_ref[...]).astype(BF16)

    def proj(lo, n):
        return _dot(h, w_ref[:, lo:lo + n])

    z = proj(P_Z, M_INNER)
    zs_ref[...] = (z * jax.nn.sigmoid(z)).reshape(nb, tl, M_INNER)

    @pl.when(t == 0)
    def _():
        xp_s[:, HIST_ROW:8, :] = hist_ref[...]

    xp_s[:, 8:8 + tl, :] = proj(P_XBC, CONV_DIM).reshape(nb, tl, CONV_DIM)
    xc = cb_ref[...] + xp_s[:, HIST_ROW:HIST_ROW + tl, :] * cw_ref[0:1, :]
    for w in range(1, CONV_W):
        xc = xc + xp_s[:, HIST_ROW + w:HIST_ROW + w + tl, :] * cw_ref[w:w + 1, :]
    xc = xc * jax.nn.sigmoid(xc)
    xs_ref[...] = xc[:, :, 0:M_INNER]
    bc_ref[...] = xc[:, :, M_INNER:CONV_DIM].astype(BF16)
    tail = xp_s[:, 8 + tl - (CONV_W - 1):8 + tl, :]
    nconv_ref[...] = tail
    xp_s[:, HIST_ROW:8, :] = tail

    dt_ref[...] = proj(P_DT, DT_PAD).reshape(nb, tl, DT_PAD)
    q_ref[...] = proj(P_Q, SB_WIDTH).astype(BF16).reshape(nb, tl, SB_WIDTH)
    for src, bf_ref, out_ref in ((P_K, k_ref, ko_ref), (P_V, v_ref, vo_ref)):
        kv = proj(src, SB_WIDTH)
        bf_ref[...] = kv.astype(BF16).reshape(nb, tl, SB_WIDTH)
        if kv_feature_major:
            out_ref[0] = kv.T.reshape(SB_HEADS, SB_HEAD_DIM, tl)
        else:
            for b in range(nb):
                for hd in range(SB_HEADS):
                    out_ref[b, hd] = kv[b * tl:(b + 1) * tl, hd * SB_HEAD_DIM:(hd + 1) * SB_HEAD_DIM]
    ga_ref[...] = proj(P_GA, D_MODEL).reshape(nb, tl, D_MODEL)
    gb_ref[...] = proj(P_GB, D_MODEL).reshape(nb, tl, D_MODEL)


def _inproj(x, hist, g, w, cw, cb, *, nb, tl, kv_feature_major):
    bsz, L, _ = x.shape
    grid = (bsz // nb, L // tl)

    def tok(n):
        return pl.BlockSpec((nb, tl, n), lambda b, t: (b, t, 0))

    if kv_feature_major:
        assert nb == 1
        head_shape = (bsz, SB_HEADS, SB_HEAD_DIM, L)
        head = pl.BlockSpec((1, SB_HEADS, SB_HEAD_DIM, tl), lambda b, t: (b, 0, 0, t))
    else:
        head_shape = (bsz, SB_HEADS, L, SB_HEAD_DIM)
        head = pl.BlockSpec((nb, SB_HEADS, tl, SB_HEAD_DIM), lambda b, t: (b, 0, t, 0))
    hist_spec = pl.BlockSpec((nb, CONV_W - 1, CONV_DIM), lambda b, t: (b, 0, 0))
    out_shape = (
        jax.ShapeDtypeStruct((bsz, L, M_INNER), F32),
        jax.ShapeDtypeStruct((bsz, L, M_INNER), F32),
        jax.ShapeDtypeStruct((bsz, L, BC_COLS), BF16),
        jax.ShapeDtypeStruct((bsz, L, DT_PAD), F32),
        jax.ShapeDtypeStruct((bsz, L, SB_WIDTH), BF16),
        jax.ShapeDtypeStruct((bsz, L, SB_WIDTH), BF16),
        jax.ShapeDtypeStruct((bsz, L, SB_WIDTH), BF16),
        jax.ShapeDtypeStruct(head_shape, F32),
        jax.ShapeDtypeStruct(head_shape, F32),
        jax.ShapeDtypeStruct((bsz, L, D_MODEL), F32),
        jax.ShapeDtypeStruct((bsz, L, D_MODEL), F32),
        jax.ShapeDtypeStruct((bsz, CONV_W - 1, CONV_DIM), F32),
    )
    out_specs = (tok(M_INNER), tok(M_INNER), tok(BC_COLS), tok(DT_PAD), tok(SB_WIDTH), tok(SB_WIDTH), tok(SB_WIDTH),
                 head, head, tok(D_MODEL), tok(D_MODEL), hist_spec)
    return pl.pallas_call(
        functools.partial(_inproj_kernel, nb=nb, tl=tl, kv_feature_major=kv_feature_major),
        grid=grid,
        in_specs=[tok(D_MODEL), hist_spec, _const_spec((1, D_MODEL)), _const_spec((D_MODEL, P_COLS)),
                  _const_spec((CONV_W, CONV_DIM)), _const_spec((1, CONV_DIM))],
        out_specs=out_specs,
        out_shape=out_shape,
        scratch_shapes=[pltpu.VMEM((nb, 8 + tl, CONV_DIM), F32)],
        compiler_params=pltpu.CompilerParams(
            dimension_semantics=("parallel", "arbitrary"), vmem_limit_bytes=VMEM_LIMIT),
        name="inproj",
    )(x, hist, g, w, cw, cb)


PAIRS = M_HEADS // 2
HEADS_PER_GROUP = M_HEADS // M_GROUPS
GROUP_COLS = HEADS_PER_GROUP * M_HEAD_DIM


def _ssd_kernel(xs_ref, bc_ref, zs_ref, dt_ref, h0_ref, dtb_ref, alog_ref, dsk_ref, mg_ref, ex_ref,
                y_ref, nssm_ref, h_s, *, q):
    t = pl.program_id(1)
    nt = pl.num_programs(1)

    @pl.when(t == 0)
    def _():
        for i in range(PAIRS):
            h_s[:, i * LANES:(i + 1) * LANES] = h0_ref[0, i].T

    xs = xs_ref[0]
    bm = bc_ref[0, :, 0:M_GROUPS * M_STATE]
    cm = bc_ref[0, :, M_GROUPS * M_STATE:BC_COLS]
    xdt = dt_ref[0] + dtb_ref[...]
    dt = jnp.maximum(xdt, 0.0) + jnp.log1p(jnp.exp(-jnp.abs(xdt)))
    da = dt * (-jnp.exp(alog_ref[...]))

    ri = lax.broadcasted_iota(jnp.int32, (q, q), 0)
    ci = lax.broadcasted_iota(jnp.int32, (q, q), 1)
    causal = ri >= ci
    tri = causal.astype(BF16)
    first = lax.broadcasted_iota(jnp.int32, (q, LANES), 1) < M_HEAD_DIM
    ex = ex_ref[...]

    def head_rows(a):
        if q % LANES:
            a = jnp.concatenate([a, jnp.zeros((LANES - q, LANES), F32)], axis=0)
        return a.T[:, :q]

    acum = sum(_dot(tri, p) for p in _split3(da))
    tot = acum[q - 1:q, :]
    ey = jnp.exp(acum)
    ws = jnp.exp(tot - acum) * dt
    dc = jnp.broadcast_to(jnp.exp(tot), (8, DT_PAD))
    stack_c = sum(_dot(p, ex) for p in _split2(jnp.concatenate([ey, ws, dc], axis=0)))
    ey_c = stack_c[0:q]
    ws_c = stack_c[q:2 * q]
    dc_c = stack_c[2 * q:2 * q + 1]
    acum_t = head_rows(acum)
    dt_t = head_rows(dt)

    xs_bf = xs.astype(BF16)
    xw_bf = (xs * ws_c).astype(BF16)
    h_prev = h_s[...]
    h_bf = h_prev.astype(BF16)

    ydiag, yoff, states = [], [], []
    for g in range(M_GROUPS):
        bg = bm[:, g * M_STATE:(g + 1) * M_STATE]
        cg = cm[:, g * M_STATE:(g + 1) * M_STATE]
        gcols = slice(g * GROUP_COLS, (g + 1) * GROUP_COLS)
        cb = _dot_nt(cg, bg)
        yoff.append(_dot(cg, h_bf[:, gcols]))
        states.append(_dot_tn(bg, xw_bf[:, gcols]))
        for pr in range(HEADS_PER_GROUP // 2):
            ms = []
            for r in range(2):
                hd = g * HEADS_PER_GROUP + pr * 2 + r
                acol = jnp.broadcast_to(acum[:, hd:hd + 1], (q, q))
                arow = jnp.broadcast_to(acum_t[hd:hd + 1, :], (q, q))
                dec = jnp.exp(jnp.where(causal, acol - arow, -1e30))
                ms.append((cb * dec * jnp.broadcast_to(dt_t[hd:hd + 1, :], (q, q))).astype(BF16))
            pair = g * (HEADS_PER_GROUP // 2) + pr
            xpair = xs_bf[:, pair * LANES:(pair + 1) * LANES]
            if q % LANES == 0:
                ydiag.append(_dot(jnp.concatenate(ms, axis=1), jnp.concatenate(_split_heads(xpair, first), axis=0)))
            else:
                ydiag.append(jnp.where(first, _dot(ms[0], xpair), _dot(ms[1], xpair)))
    y = jnp.concatenate(ydiag, axis=1) + jnp.concatenate(yoff, axis=1) * ey_c + dsk_ref[...] * xs
    y = y * zs_ref[0]
    normed = []
    for g in range(M_GROUPS):
        yg = y[:, g * GROUP_COLS:(g + 1) * GROUP_COLS]
        normed.append(yg * lax.rsqrt(jnp.mean(yg * yg, axis=-1, keepdims=True) + EPS))
    y_ref[0] = (jnp.concatenate(normed, axis=1) * mg_ref[...]).astype(BF16)
    h_s[...] = h_prev * dc_c + jnp.concatenate(states, axis=1)

    @pl.when(t == nt - 1)
    def _():
        for i in range(PAIRS):
            nssm_ref[0, i] = h_s[:, i * LANES:(i + 1) * LANES].T


def _ssd(xs, bc, zs, dt, h0, dtb, alog, dsk, mg, ex, *, q):
    bsz, L, _ = xs.shape
    grid = (bsz, L // q)
    h0p = h0.reshape(bsz, PAIRS, LANES, M_STATE)

    def tok(n):
        return pl.BlockSpec((1, q, n), lambda b, t: (b, t, 0))

    per_b = pl.BlockSpec((1, PAIRS, LANES, M_STATE), lambda b, t: (b, 0, 0, 0))
    y, nssm = pl.pallas_call(
        functools.partial(_ssd_kernel, q=q),
        grid=grid,
        in_specs=[tok(M_INNER), tok(BC_COLS), tok(M_INNER), tok(DT_PAD), per_b,
                  _const_spec((1, DT_PAD)), _const_spec((1, DT_PAD)), _const_spec((1, M_INNER)),
                  _const_spec((1, M_INNER)), _const_spec((DT_PAD, M_INNER))],
        out_specs=(tok(M_INNER), per_b),
        out_shape=(jax.ShapeDtypeStruct((bsz, L, M_INNER), BF16),
                   jax.ShapeDtypeStruct((bsz, PAIRS, LANES, M_STATE), F32)),
        scratch_shapes=[pltpu.VMEM((M_STATE, M_INNER), F32)],
        compiler_params=pltpu.CompilerParams(
            dimension_semantics=("parallel", "arbitrary"), vmem_limit_bytes=VMEM_LIMIT),
        name="ssd",
    )(xs, bc, zs, dt, h0p, dtb, alog, dsk, mg, ex)
    return y, nssm.reshape(bsz, M_HEADS, M_HEAD_DIM, M_STATE)


def _sb_weights(z, tri, carry, mask):
    lk = -(jnp.maximum(z, 0.0) + jnp.log(1.0 + jnp.exp(-jnp.abs(z))))
    if mask is not None:
        lk = jnp.where(mask, lk, 0.0)
    hi, lo = _split2(lk)
    cs = _dot(hi, tri) + _dot(lo, tri)
    w = jnp.exp(z + cs + carry)
    if mask is not None:
        w = jnp.where(mask, w, 0.0)
    return w.astype(BF16), carry + cs[:, 0:1]


def _suffix_tri(n):
    r = lax.broadcasted_iota(jnp.int32, (n, n), 0)
    c = lax.broadcasted_iota(jnp.int32, (n, n), 1)
    return (r >= c).astype(BF16), c < r


def _sb_prompt_kernel(q_ref, k_ref, v_ref, o_ref, acc_s, c_s, *, tb):
    qi = pl.program_id(2)
    first = lax.broadcasted_iota(jnp.int32, (tb, LANES), 1) < SB_HEAD_DIM
    qh = _split_heads(q_ref[0], first)
    tri, diag_mask = _suffix_tri(tb)

    def step(j, mask):
        rows = pl.ds(pl.multiple_of(j * tb, tb), tb)
        k2 = k_ref[0, rows, :]
        ws = []
        live = None
        for h in range(2):
            carry = jnp.zeros((tb, 1), F32) if mask is not None else c_s[h]
            w, carry = _sb_weights(_dot_nt(qh[h], k2), tri, carry, mask)
            c_s[h] = carry
            ws.append(w)
            m = jnp.max(carry)
            live = m if live is None else jnp.maximum(live, m)
        pv = _dot(jnp.concatenate(ws, axis=1), jnp.concatenate(_split_heads(v_ref[0, rows, :], first), axis=0))
        if mask is not None:
            acc_s[...] = pv
        else:
            acc_s[...] += pv
        return live

    def cond(st):
        j, live = st
        return jnp.logical_and(j >= 0, live > SB_LOG_CUTOFF)

    def body(st):
        j, _ = st
        return j - 1, step(j, None)

    lax.while_loop(cond, body, (qi - 1, step(qi, diag_mask)))
    o_ref[0] = acc_s[...].astype(BF16)


def _sb_prompt(q, k, v, *, tb):
    bsz, L, _ = q.shape
    grid = (bsz, SB_HEADS // 2, L // tb)
    qspec = pl.BlockSpec((1, tb, LANES), lambda b, h, i: (b, i, h))
    kspec = pl.BlockSpec((1, L, LANES), lambda b, h, i: (b, 0, h))
    return pl.pallas_call(
        functools.partial(_sb_prompt_kernel, tb=tb),
        grid=grid,
        in_specs=[qspec, kspec, kspec],
        out_specs=qspec,
        out_shape=jax.ShapeDtypeStruct((bsz, L, SB_WIDTH), BF16),
        scratch_shapes=[pltpu.VMEM((tb, LANES), F32), pltpu.VMEM((2, tb, 1), F32)],
        compiler_params=pltpu.CompilerParams(
            dimension_semantics=("parallel", "parallel", "arbitrary"), vmem_limit_bytes=VMEM_LIMIT),
        name="sb_prompt",
    )(q, k, v)


def _sb_sample_kernel(q_ref, k_ref, v_ref, pk_ref, pv_ref, o_ref, acc_s, *, tq, tk, npast):
    q2 = q_ref[0]
    first = lax.broadcasted_iota(jnp.int32, (tq, LANES), 1) < SB_HEAD_DIM
    qh = _split_heads(q2, first)
    tri_d, diag_mask = _suffix_tri(tq)
    tri_p, _ = _suffix_tri(tk)
    ws, carries = [], []
    for h in range(2):
        w, c = _sb_weights(_dot_nt(qh[h], k_ref[0]), tri_d, jnp.zeros((tq, 1), F32), diag_mask)
        ws.append(w)
        carries.append(c)
    acc_s[...] = _dot(jnp.concatenate(ws, axis=1), jnp.concatenate(_split_heads(v_ref[0], first), axis=0))

    pr = lax.broadcasted_iota(jnp.int32, (LANES, SB_HEAD_DIM), 0)
    pc = lax.broadcasted_iota(jnp.int32, (LANES, SB_HEAD_DIM), 1)
    for h in range(2):
        place = (pr == pc + h * SB_HEAD_DIM).astype(BF16)
        qs = q2[:, h * SB_HEAD_DIM:(h + 1) * SB_HEAD_DIM]

        def cond(st):
            j, live, _ = st
            return jnp.logical_and(j >= 0, live > SB_LOG_CUTOFF)

        def body(st, h=h, qs=qs, place=place):
            j, _, c = st
            cols = pl.ds(pl.multiple_of(j * tk, tk), tk)
            w, c = _sb_weights(_dot(qs, pk_ref[0, h, :, cols].astype(BF16)), tri_p, c, None)
            vt = _dot(place, pv_ref[0, h, :, cols].astype(BF16)).astype(BF16)
            acc_s[...] += _dot_nt(w, vt)
            return j - 1, jnp.max(c), c

        lax.while_loop(cond, body, (jnp.int32(npast - 1), jnp.max(carries[h]), carries[h]))
    o_ref[0] = acc_s[...].astype(BF16)


def _sb_sample(q, k, v, past_kt, past_vt, *, tk):
    bsz, tq, _ = q.shape
    past = past_kt.shape[3]
    grid = (bsz, SB_HEADS // 2)
    qspec = pl.BlockSpec((1, tq, LANES), lambda b, h: (b, 0, h))
    pspec = pl.BlockSpec((1, 2, SB_HEAD_DIM, past), lambda b, h: (b, h, 0, 0))
    return pl.pallas_call(
        functools.partial(_sb_sample_kernel, tq=tq, tk=tk, npast=past // tk),
        grid=grid,
        in_specs=[qspec, qspec, qspec, pspec, pspec],
        out_specs=qspec,
        out_shape=jax.ShapeDtypeStruct((bsz, tq, SB_WIDTH), BF16),
        scratch_shapes=[pltpu.VMEM((tq, LANES), F32)],
        compiler_params=pltpu.CompilerParams(
            dimension_semantics=("parallel", "parallel"), vmem_limit_bytes=VMEM_LIMIT),
        name="sb_sample",
    )(q, k, v, past_kt, past_vt)


def _post_kernel(x_ref, y_ref, o_ref, ga_ref, gb_ref, p_ref, wa_ref, wb_ref, wo_ref, gf_ref, wg_ref, wu_ref,
                 wd_ref, gp_ref, wpg_ref, wp_ref, gl_ref, out_ref, *, final_norm):
    branch_a = _dot(y_ref[...], wa_ref[...])
    branch_b = _dot(o_ref[...], wb_ref[...])
    merged = jax.nn.sigmoid(ga_ref[...]) * branch_a + jax.nn.sigmoid(gb_ref[...]) * branch_b
    x = x_ref[...] + _dot(merged.astype(BF16), wo_ref[...])
    hf = _rms(x, gf_ref[...]).astype(BF16)
    gate = _dot(hf, wg_ref[...])
    act = (gate * jax.nn.sigmoid(gate) * _dot(hf, wu_ref[...])).astype(BF16)
    x = x + _dot(act, wd_ref[...])
    g = jax.nn.sigmoid(_dot(_rms(x, gp_ref[...]).astype(BF16), wpg_ref[...]))
    x = x + g * _dot(p_ref[...].astype(BF16), wp_ref[...])
    if final_norm:
        x = _rms(x, gl_ref[...])
    out_ref[...] = x


def _post(x, y, o, ga, gb, p, wts, *, tm, final_norm):
    T = x.shape[0]

    def tok(n):
        return pl.BlockSpec((tm, n), lambda i: (i, 0))

    wspecs = [_const_spec(w.shape) for w in wts]
    return pl.pallas_call(
        functools.partial(_post_kernel, final_norm=final_norm),
        grid=(T // tm,),
        in_specs=[tok(D_MODEL), tok(M_INNER), tok(SB_WIDTH), tok(D_MODEL), tok(D_MODEL), tok(PLE_DIM)] + wspecs,
        out_specs=tok(D_MODEL),
        out_shape=jax.ShapeDtypeStruct((T, D_MODEL), F32),
        compiler_params=pltpu.CompilerParams(
            dimension_semantics=("parallel",), vmem_limit_bytes=VMEM_LIMIT),
        name="post",
    )(x, y, o, ga, gb, p, *wts)


def _pack_w_in(w_in):
    dt_cols = jnp.pad(w_in[:, OFF_DT:OFF_Q], ((0, 0), (0, DT_PAD - M_HEADS)))
    parts = [w_in[:, OFF_Z:OFF_DT], dt_cols, w_in[:, OFF_Q:OFF_K] * SB_SCALE, w_in[:, OFF_K:IN_COLS]]
    return jnp.concatenate(parts, axis=1).astype(BF16)


def _head_expand():
    r = lax.broadcasted_iota(jnp.int32, (DT_PAD, M_INNER), 0)
    c = lax.broadcasted_iota(jnp.int32, (DT_PAD, M_INNER), 1)
    return (c // M_HEAD_DIM == r).astype(BF16)


def _pad_heads(v):
    return jnp.pad(v.reshape(1, M_HEADS).astype(F32), ((0, 0), (0, DT_PAD - M_HEADS)))


def _layer(x, p, past_k, past_v, conv_hist, h0, prm, *, final_norm):
    bsz, L, _ = x.shape
    prompt = past_k is None
    if prompt:
        nb, tl = 1, min(L, TOKEN_TILE)
        conv_hist = jnp.zeros((bsz, CONV_W - 1, CONV_DIM), F32)
        h0 = jnp.zeros((bsz, M_HEADS, M_HEAD_DIM, M_STATE), F32)
    else:
        nb, tl = bsz, L
    zs, xs, bc, dt, q, k, v, new_k, new_v, ga, gb, new_conv = _inproj(
        x, conv_hist.astype(F32), prm["norm_mix_g"], prm["w_in"], prm["conv_w"], prm["conv_b"],
        nb=nb, tl=tl, kv_feature_major=prompt)
    if prompt:
        new_k, new_v = jnp.swapaxes(new_k, 2, 3), jnp.swapaxes(new_v, 2, 3)

    y, new_ssm = _ssd(xs, bc, zs, dt, h0.astype(F32), prm["dt_bias"], prm["a_log"], prm["d_skip"], prm["mnorm_g"],
                      prm["head_expand"], q=min(L, SSD_CHUNK))
    if prompt:
        o = _sb_prompt(q, k, v, tb=min(L, SB_BLOCK))
    else:
        o = _sb_sample(q, k, v, jnp.swapaxes(past_k.astype(F32), 2, 3), jnp.swapaxes(past_v.astype(F32), 2, 3),
                       tk=min(past_k.shape[2], SB_BLOCK))

    T = bsz * L
    flat = lambda a: a.reshape(T, a.shape[-1])
    x = _post(flat(x), flat(y), flat(o), flat(ga), flat(gb), flat(p), prm["post"], tm=min(T, TOKEN_TILE),
              final_norm=final_norm)
    return x.reshape(bsz, L, D_MODEL), new_k, new_v, new_conv, new_ssm


def _layer_params(i, norm_mix_g, w_in, conv_w, conv_b, dt_bias, a_log, d_skip, mnorm_g, w_a, w_b, w_out,
                  norm_ffn_g, w_gate, w_up, w_down, norm_ple_g, w_ple_gate, w_ple, final_norm_g):
    row = lambda v: v.reshape(1, -1).astype(F32)
    bf = lambda w: w.astype(BF16)
    return {
        "norm_mix_g": row(norm_mix_g[i]),
        "w_in": _pack_w_in(w_in[i]),
        "conv_w": conv_w[i].astype(F32),
        "conv_b": row(conv_b[i]),
        "dt_bias": _pad_heads(dt_bias[i]),
        "a_log": _pad_heads(a_log[i]),
        "d_skip": row(jnp.repeat(d_skip[i], M_HEAD_DIM)),
        "mnorm_g": row(mnorm_g[i]),
        "head_expand": _head_expand(),
        "post": (bf(w_a[i]), bf(w_b[i]), bf(w_out[i]), row(norm_ffn_g[i]), bf(w_gate[i]), bf(w_up[i]),
                 bf(w_down[i]), row(norm_ple_g[i]), bf(w_ple_gate[i]), bf(w_ple[i]), row(final_norm_g)),
    }


def kernel(x_prompt, x_sample, cache_k, cache_v, state_conv, state_ssm, p_prompt, p_sample, norm_mix_g, w_in,
           conv_w, conv_b, dt_bias, a_log, d_skip, mnorm_g, w_a, w_b, w_out, norm_ffn_g, w_gate, w_up, w_down,
           norm_ple_g, w_ple_gate, w_ple, final_norm_g):
    depth = w_in.shape[0]
    xp, xs = x_prompt, x_sample
    outs_p, outs_s = [], []
    for i in range(depth):
        prm = _layer_params(i, norm_mix_g, w_in, conv_w, conv_b, dt_bias, a_log, d_skip, mnorm_g, w_a, w_b,
                            w_out, norm_ffn_g, w_gate, w_up, w_down, norm_ple_g, w_ple_gate, w_ple, final_norm_g)
        last = i == depth - 1
        xp, *st_p = _layer(xp, p_prompt[i], None, None, None, None, prm, final_norm=last)
        xs, *st_s = _layer(xs, p_sample[i], cache_k[i], cache_v[i], state_conv[i], state_ssm[i], prm,
                           final_norm=last)
        outs_p.append(st_p)
        outs_s.append(st_s)
    stack = lambda outs, j: jnp.stack([o[j] for o in outs])
    return (xp, xs,
            stack(outs_p, 0), stack(outs_p, 1), stack(outs_p, 2), stack(outs_p, 3),
            stack(outs_s, 0), stack(outs_s, 1), stack(outs_s, 2), stack(outs_s, 3))
```

```python
import functools

import jax
import jax.numpy as jnp
from jax import lax
from jax.experimental import pallas as pl
from jax.experimental.pallas import tpu as pltpu

F32 = jnp.float32
BF16 = jnp.bfloat16

EPS = 1e-6
D_MODEL = 1024
M_HEAD_DIM = 64
M_HEADS = 16
M_GROUPS = 4
M_STATE = 128
M_INNER = M_HEADS * M_HEAD_DIM
CONV_W = 4
CONV_DIM = M_INNER + 2 * M_GROUPS * M_STATE
SB_HEADS = 16
SB_HEAD_DIM = 64
SB_WIDTH = SB_HEADS * SB_HEAD_DIM
SB_SCALE = SB_HEAD_DIM ** -0.5
D_FF = 2816
PLE_DIM = 256
OFF_Z = 0
OFF_XBC = OFF_Z + M_INNER
OFF_DT = OFF_XBC + CONV_DIM
OFF_Q = OFF_DT + M_HEADS
OFF_K = OFF_Q + SB_WIDTH
OFF_V = OFF_K + SB_WIDTH
OFF_GA = OFF_V + SB_WIDTH
OFF_GB = OFF_GA + D_MODEL
IN_COLS = OFF_GB + D_MODEL

LANES = 128
DT_PAD = LANES
P_Z = 0
P_XBC = P_Z + M_INNER
P_DT = P_XBC + CONV_DIM
P_Q = P_DT + DT_PAD
P_K = P_Q + SB_WIDTH
P_V = P_K + SB_WIDTH
P_GA = P_V + SB_WIDTH
P_GB = P_GA + D_MODEL
P_COLS = P_GB + D_MODEL

VMEM_LIMIT = 56 * 1024 * 1024
LOG2_E = 1.4426950408889634
SB_LOG_CUTOFF = -150.0
TOKEN_TILE = 256
SB_BLOCK = 256
SSD_CHUNK = 256


def _const_spec(shape):
    nd = len(shape)
    return pl.BlockSpec(shape, lambda *_: (0,) * nd, pipeline_mode=pl.Buffered(1))


def _rms(x, g):
    ms = jnp.mean(x * x, axis=-1, keepdims=True)
    return x * lax.rsqrt(ms + EPS) * g


def _split2(x):
    hi = x.astype(BF16)
    lo = (x - hi.astype(F32)).astype(BF16)
    return hi, lo


def _split3(x):
    hi = x.astype(BF16)
    r = x - hi.astype(F32)
    mid = r.astype(BF16)
    lo = (r - mid.astype(F32)).astype(BF16)
    return hi, mid, lo


def _dot(a, b):
    return jnp.dot(a, b, preferred_element_type=F32)


def _dot_nt(a, b):
    return lax.dot_general(a, b, (((1,), (1,)), ((), ())), preferred_element_type=F32)


def _dot_tn(a, b):
    return lax.dot_general(a, b, (((0,), (0,)), ((), ())), preferred_element_type=F32)


def _split_heads(x2, first):
    zero = jnp.zeros_like(x2)
    return jnp.where(first, x2, zero), jnp.where(first, zero, x2)


HIST_ROW = 8 - (CONV_W - 1)
BC_COLS = 2 * M_GROUPS * M_STATE
CONV_STRIP = 512


def _inproj_kernel(x_ref, hist_ref, g_ref, w_ref, cw_ref, cb_ref, zs_ref, xs_ref, bc_ref, dt_ref, q_ref, k_ref,
                   v_ref, ko_ref, vo_ref, ga_ref, gb_ref, nconv_ref, xp_s, *, nb, tl, kv_feature_major):
    t = pl.program_id(1)
    tm = nb * tl
    x = x_ref[...].reshape(tm, D_MODEL)
    h = _rms(x, g_ref[...]).astype(BF16)

    def proj(lo, n):
        return _dot(h, w_ref[:, lo:lo + n])

    @pl.when(t == 0)
    def _():
        xp_s[:, HIST_ROW:8, :] = hist_ref[...]

    def conv_cols(c0):
        cols = slice(c0, c0 + CONV_STRIP)
        xp_s[:, 8:8 + tl, cols] = proj(P_XBC + c0, CONV_STRIP).reshape(nb, tl, CONV_STRIP)
        xc = cb_ref[:, cols] + xp_s[:, HIST_ROW:HIST_ROW + tl, cols] * cw_ref[0:1, cols]
        for w in range(1, CONV_W):
            xc = xc + xp_s[:, HIST_ROW + w:HIST_ROW + w + tl, cols] * cw_ref[w:w + 1, cols]
        xc = xc * jax.nn.sigmoid(xc)
        if c0 < M_INNER:
            xs_ref[:, :, cols] = xc
        else:
            bc_ref[:, :, c0 - M_INNER:c0 - M_INNER + CONV_STRIP] = xc.astype(BF16)
        tail = xp_s[:, 8 + tl - (CONV_W - 1):8 + tl, cols]
        nconv_ref[:, :, cols] = tail
        xp_s[:, HIST_ROW:8, cols] = tail

    def kv_proj(src, bf_ref, out_ref):
        kv = proj(src, SB_WIDTH)
        bf_ref[...] = kv.astype(BF16).reshape(nb, tl, SB_WIDTH)
        if kv_feature_major:
            out_ref[0] = kv.T.reshape(SB_HEADS, SB_HEAD_DIM, tl)
        else:
            for b in range(nb):
                for hd in range(SB_HEADS):
                    out_ref[b, hd] = kv[b * tl:(b + 1) * tl, hd * SB_HEAD_DIM:(hd + 1) * SB_HEAD_DIM]

    def z_proj():
        z = proj(P_Z, M_INNER)
        zs_ref[...] = (z * jax.nn.sigmoid(z)).reshape(nb, tl, M_INNER)

    def q_proj():
        dt_ref[...] = proj(P_DT, DT_PAD).reshape(nb, tl, DT_PAD)
        q_ref[...] = proj(P_Q, SB_WIDTH).astype(BF16).reshape(nb, tl, SB_WIDTH)

    def gate_proj(src, out_ref):
        out_ref[...] = proj(src, D_MODEL).reshape(nb, tl, D_MODEL)

    others = [z_proj, q_proj, functools.partial(kv_proj, P_K, k_ref, ko_ref),
              functools.partial(kv_proj, P_V, v_ref, vo_ref), functools.partial(gate_proj, P_GA, ga_ref),
              functools.partial(gate_proj, P_GB, gb_ref)]
    strips = list(range(0, CONV_DIM, CONV_STRIP))
    for i in range(max(len(others), len(strips))):
        if i < len(strips):
            conv_cols(strips[i])
        if i < len(others):
            others[i]()


def _inproj(x, hist, g, w, cw, cb, *, nb, tl, kv_feature_major):
    bsz, L, _ = x.shape
    grid = (bsz // nb, L // tl)

    def tok(n):
        return pl.BlockSpec((nb, tl, n), lambda b, t: (b, t, 0))

    if kv_feature_major:
        assert nb == 1
        head_shape = (bsz, SB_HEADS, SB_HEAD_DIM, L)
        head = pl.BlockSpec((1, SB_HEADS, SB_HEAD_DIM, tl), lambda b, t: (b, 0, 0, t))
    else:
        head_shape = (bsz, SB_HEADS, L, SB_HEAD_DIM)
        head = pl.BlockSpec((nb, SB_HEADS, tl, SB_HEAD_DIM), lambda b, t: (b, 0, t, 0))
    hist_spec = pl.BlockSpec((nb, CONV_W - 1, CONV_DIM), lambda b, t: (b, 0, 0))
    out_shape = (
        jax.ShapeDtypeStruct((bsz, L, M_INNER), F32),
        jax.ShapeDtypeStruct((bsz, L, M_INNER), F32),
        jax.ShapeDtypeStruct((bsz, L, BC_COLS), BF16),
        jax.ShapeDtypeStruct((bsz, L, DT_PAD), F32),
        jax.ShapeDtypeStruct((bsz, L, SB_WIDTH), BF16),
        jax.ShapeDtypeStruct((bsz, L, SB_WIDTH), BF16),
        jax.ShapeDtypeStruct((bsz, L, SB_WIDTH), BF16),
        jax.ShapeDtypeStruct(head_shape, F32),
        jax.ShapeDtypeStruct(head_shape, F32),
        jax.ShapeDtypeStruct((bsz, L, D_MODEL), F32),
        jax.ShapeDtypeStruct((bsz, L, D_MODEL), F32),
        jax.ShapeDtypeStruct((bsz, CONV_W - 1, CONV_DIM), F32),
    )
    out_specs = (tok(M_INNER), tok(M_INNER), tok(BC_COLS), tok(DT_PAD), tok(SB_WIDTH), tok(SB_WIDTH), tok(SB_WIDTH),
                 head, head, tok(D_MODEL), tok(D_MODEL), hist_spec)
    return pl.pallas_call(
        functools.partial(_inproj_kernel, nb=nb, tl=tl, kv_feature_major=kv_feature_major),
        grid=grid,
        in_specs=[tok(D_MODEL), hist_spec, _const_spec((1, D_MODEL)), _const_spec((D_MODEL, P_COLS)),
                  _const_spec((CONV_W, CONV_DIM)), _const_spec((1, CONV_DIM))],
        out_specs=out_specs,
        out_shape=out_shape,
        scratch_shapes=[pltpu.VMEM((nb, 8 + tl, CONV_DIM), F32)],
        compiler_params=pltpu.CompilerParams(
            dimension_semantics=("parallel", "arbitrary"), vmem_limit_bytes=VMEM_LIMIT),
        name="inproj",
    )(x, hist, g, w, cw, cb)


PAIRS = M_HEADS // 2
HEADS_PER_GROUP = M_HEADS // M_GROUPS
GROUP_COLS = HEADS_PER_GROUP * M_HEAD_DIM


def _ssd_kernel(xs_ref, bc_ref, zs_ref, dt_ref, h0_ref, dtb_ref, alog_ref, dsk_ref, mg_ref, ex_ref,
                y_ref, nssm_ref, h_s, *, q):
    t = pl.program_id(1)
    nt = pl.num_programs(1)

    @pl.when(t == 0)
    def _():
        for i in range(PAIRS):
            h_s[:, i * LANES:(i + 1) * LANES] = h0_ref[0, i].T

    xs = xs_ref[0]
    bm = bc_ref[0, :, 0:M_GROUPS * M_STATE]
    cm = bc_ref[0, :, M_GROUPS * M_STATE:BC_COLS]
    xdt = dt_ref[0] + dtb_ref[...]
    dt = jnp.maximum(xdt, 0.0) + jnp.log1p(jnp.exp(-jnp.abs(xdt)))
    da = dt * (-jnp.exp(alog_ref[...]))

    ri = lax.broadcasted_iota(jnp.int32, (q, q), 0)
    ci = lax.broadcasted_iota(jnp.int32, (q, q), 1)
    causal = ri >= ci
    tri = causal.astype(BF16)
    first = lax.broadcasted_iota(jnp.int32, (q, LANES), 1) < M_HEAD_DIM
    ex = ex_ref[...]

    def head_rows(a):
        if q % LANES:
            a = jnp.concatenate([a, jnp.zeros((LANES - q, LANES), F32)], axis=0)
        return a.T[:, :q]

    acum = sum(_dot(tri, p) for p in _split3(da))
    tot = acum[q - 1:q, :]
    ey = jnp.exp(acum)
    ws = jnp.exp(tot - acum) * dt
    dc = jnp.broadcast_to(jnp.exp(tot), (8, DT_PAD))
    stack_c = sum(_dot(p, ex) for p in _split2(jnp.concatenate([ey, ws, dc], axis=0)))
    ey_c = stack_c[0:q]
    ws_c = stack_c[q:2 * q]
    dc_c = stack_c[2 * q:2 * q + 1]
    acum_t = head_rows(acum)
    dt_t = head_rows(dt)

    xs_bf = xs.astype(BF16)
    xw_bf = (xs * ws_c).astype(BF16)
    h_prev = h_s[...]
    h_bf = h_prev.astype(BF16)

    ydiag, yoff, states = [], [], []
    for g in range(M_GROUPS):
        bg = bm[:, g * M_STATE:(g + 1) * M_STATE]
        cg = cm[:, g * M_STATE:(g + 1) * M_STATE]
        gcols = slice(g * GROUP_COLS, (g + 1) * GROUP_COLS)
        cb = _dot_nt(cg, bg)
        yoff.append(_dot(cg, h_bf[:, gcols]))
        states.append(_dot_tn(bg, xw_bf[:, gcols]))
        for pr in range(HEADS_PER_GROUP // 2):
            ms = []
            for r in range(2):
                hd = g * HEADS_PER_GROUP + pr * 2 + r
                acol = jnp.broadcast_to(acum[:, hd:hd + 1], (q, q))
                arow = jnp.broadcast_to(acum_t[hd:hd + 1, :], (q, q))
                dec = jnp.exp(jnp.where(causal, acol - arow, -1e30))
                ms.append((cb * dec * jnp.broadcast_to(dt_t[hd:hd + 1, :], (q, q))).astype(BF16))
            pair = g * (HEADS_PER_GROUP // 2) + pr
            xpair = xs_bf[:, pair * LANES:(pair + 1) * LANES]
            if q % LANES == 0:
                ydiag.append(_dot(jnp.concatenate(ms, axis=1), jnp.concatenate(_split_heads(xpair, first), axis=0)))
            else:
                ydiag.append(jnp.where(first, _dot(ms[0], xpair), _dot(ms[1], xpair)))
    y = jnp.concatenate(ydiag, axis=1) + jnp.concatenate(yoff, axis=1) * ey_c + dsk_ref[...] * xs
    y = y * zs_ref[0]
    normed = []
    for g in range(M_GROUPS):
        yg = y[:, g * GROUP_COLS:(g + 1) * GROUP_COLS]
        normed.append(yg * lax.rsqrt(jnp.mean(yg * yg, axis=-1, keepdims=True) + EPS))
    y_ref[0] = (jnp.concatenate(normed, axis=1) * mg_ref[...]).astype(BF16)
    h_s[...] = h_prev * dc_c + jnp.concatenate(states, axis=1)

    @pl.when(t == nt - 1)
    def _():
        for i in range(PAIRS):
            nssm_ref[0, i] = h_s[:, i * LANES:(i + 1) * LANES].T


def _ssd(xs, bc, zs, dt, h0, dtb, alog, dsk, mg, ex, *, q):
    bsz, L, _ = xs.shape
    grid = (bsz, L // q)
    h0p = h0.reshape(bsz, PAIRS, LANES, M_STATE)

    def tok(n):
        return pl.BlockSpec((1, q, n), lambda b, t: (b, t, 0))

    per_b = pl.BlockSpec((1, PAIRS, LANES, M_STATE), lambda b, t: (b, 0, 0, 0))
    y, nssm = pl.pallas_call(
        functools.partial(_ssd_kernel, q=q),
        grid=grid,
        in_specs=[tok(M_INNER), tok(BC_COLS), tok(M_INNER), tok(DT_PAD), per_b,
                  _const_spec((1, DT_PAD)), _const_spec((1, DT_PAD)), _const_spec((1, M_INNER)),
                  _const_spec((1, M_INNER)), _const_spec((DT_PAD, M_INNER))],
        out_specs=(tok(M_INNER), per_b),
        out_shape=(jax.ShapeDtypeStruct((bsz, L, M_INNER), BF16),
                   jax.ShapeDtypeStruct((bsz, PAIRS, LANES, M_STATE), F32)),
        scratch_shapes=[pltpu.VMEM((M_STATE, M_INNER), F32)],
        compiler_params=pltpu.CompilerParams(
            dimension_semantics=("parallel", "arbitrary"), vmem_limit_bytes=VMEM_LIMIT),
        name="ssd",
    )(xs, bc, zs, dt, h0p, dtb, alog, dsk, mg, ex)
    return y, nssm.reshape(bsz, M_HEADS, M_HEAD_DIM, M_STATE)


def _sb_weights(z, tri, carry, mask):
    u = z * LOG2_E
    neg_abs = lax.bitcast_convert_type(lax.bitcast_convert_type(u, jnp.uint32) | jnp.uint32(0x80000000), F32)
    drop = jnp.maximum(u, 0.0) + jnp.log2(1.0 + jnp.exp2(neg_abs))
    if mask is not None:
        drop = jnp.where(mask, drop, 0.0)
    cs = _dot(drop.astype(BF16), tri)
    w = jnp.exp2(u - cs + carry)
    if mask is not None:
        w = jnp.where(mask, w, 0.0)
    return w.astype(BF16), carry - cs[:, 0:1]


def _suffix_tri(n):
    r = lax.broadcasted_iota(jnp.int32, (n, n), 0)
    c = lax.broadcasted_iota(jnp.int32, (n, n), 1)
    return (r >= c).astype(BF16), c < r


SB_STEP_HEADS = 4


def _sb_prompt_kernel(q_ref, k_ref, v_ref, o_ref, acc_s, c_s, *, tb):
    qi = pl.program_id(2)
    nh = SB_STEP_HEADS
    width = nh * SB_HEAD_DIM
    first = lax.broadcasted_iota(jnp.int32, (tb, LANES), 1) < SB_HEAD_DIM
    lane_head = lax.broadcasted_iota(jnp.int32, (tb, width), 1) // SB_HEAD_DIM
    tri, diag_mask = _suffix_tri(tb)
    qh = []
    for p in range(nh // 2):
        qh += list(_split_heads(q_ref[0, :, p * LANES:(p + 1) * LANES], first))

    def weights(j, mask, carries):
        rows = pl.ds(pl.multiple_of(j * tb, tb), tb)
        ws, cs = [], []
        for i in range(nh):
            k2 = k_ref[0, rows, (i // 2) * LANES:(i // 2 + 1) * LANES]
            w, c = _sb_weights(_dot_nt(qh[i], k2), tri, carries[i], mask)
            ws.append(w)
            cs.append(c)
        return ws, cs

    def values(j):
        v = v_ref[0, pl.ds(pl.multiple_of(j * tb, tb), tb), :]
        zero = jnp.zeros_like(v)
        return [jnp.where(lane_head == i, v, zero) for i in range(nh)]

    def pv(ws, vs):
        return _dot(jnp.concatenate(ws, axis=1), jnp.concatenate(vs, axis=0))

    def keep(cs):
        for i in range(nh):
            c_s[i] = cs[i]

    zero = [jnp.zeros((tb, 1), F32)] * nh

    @pl.when(qi == 0)
    def _():
        ws, cs = weights(0, diag_mask, zero)
        acc_s[...] = pv(ws, values(0))
        keep(cs)

    @pl.when(qi > 0)
    def _():
        ws_d, cs = weights(qi, diag_mask, zero)
        ws_p, cs = weights(qi - 1, None, cs)
        acc_s[...] = pv(ws_d + ws_p, values(qi) + values(qi - 1))
        keep(cs)

    def live():
        return functools.reduce(jnp.maximum, [jnp.max(c_s[i]) for i in range(nh)])

    def cond(st):
        j, m = st
        return jnp.logical_and(j >= 0, m > SB_LOG_CUTOFF)

    def body(st):
        j, _ = st
        ws, cs = weights(j, None, [c_s[i] for i in range(nh)])
        acc_s[...] += pv(ws, values(j))
        keep(cs)
        return j - 1, live()

    lax.while_loop(cond, body, (qi - 2, live()))
    o_ref[0] = acc_s[...].astype(BF16)


def _sb_prompt(q, k, v, *, tb):
    bsz, L, _ = q.shape
    width = SB_STEP_HEADS * SB_HEAD_DIM
    grid = (bsz, SB_HEADS // SB_STEP_HEADS, L // tb)
    qspec = pl.BlockSpec((1, tb, width), lambda b, h, i: (b, i, h))
    kspec = pl.BlockSpec((1, L, width), lambda b, h, i: (b, 0, h))
    return pl.pallas_call(
        functools.partial(_sb_prompt_kernel, tb=tb),
        grid=grid,
        in_specs=[qspec, kspec, kspec],
        out_specs=qspec,
        out_shape=jax.ShapeDtypeStruct((bsz, L, SB_WIDTH), BF16),
        scratch_shapes=[pltpu.VMEM((tb, width), F32), pltpu.VMEM((SB_STEP_HEADS, tb, 1), F32)],
        compiler_params=pltpu.CompilerParams(
            dimension_semantics=("parallel", "parallel", "arbitrary"), vmem_limit_bytes=VMEM_LIMIT),
        name="sb_prompt",
    )(q, k, v)


def _sb_sample_kernel(q_ref, k_ref, v_ref, pk_ref, pv_ref, o_ref, acc_s, *, tq, tk, npast):
    q2 = q_ref[0]
    first = lax.broadcasted_iota(jnp.int32, (tq, LANES), 1) < SB_HEAD_DIM
    qh = _split_heads(q2, first)
    tri_d, diag_mask = _suffix_tri(tq)
    tri_p, _ = _suffix_tri(tk)
    ws, carries = [], []
    for h in range(2):
        w, c = _sb_weights(_dot_nt(qh[h], k_ref[0]), tri_d, jnp.zeros((tq, 1), F32), diag_mask)
        ws.append(w)
        carries.append(c)
    acc_s[...] = _dot(jnp.concatenate(ws, axis=1), jnp.concatenate(_split_heads(v_ref[0], first), axis=0))

    pr = lax.broadcasted_iota(jnp.int32, (LANES, SB_HEAD_DIM), 0)
    pc = lax.broadcasted_iota(jnp.int32, (LANES, SB_HEAD_DIM), 1)
    for h in range(2):
        place = (pr == pc + h * SB_HEAD_DIM).astype(BF16)
        qs = q2[:, h * SB_HEAD_DIM:(h + 1) * SB_HEAD_DIM]

        def cond(st):
            j, live, _ = st
            return jnp.logical_and(j >= 0, live > SB_LOG_CUTOFF)

        def body(st, h=h, qs=qs, place=place):
            j, _, c = st
            cols = pl.ds(pl.multiple_of(j * tk, tk), tk)
            w, c = _sb_weights(_dot(qs, pk_ref[0, h, :, cols].astype(BF16)), tri_p, c, None)
            vt = _dot(place, pv_ref[0, h, :, cols].astype(BF16)).astype(BF16)
            acc_s[...] += _dot_nt(w, vt)
            return j - 1, jnp.max(c), c

        lax.while_loop(cond, body, (jnp.int32(npast - 1), jnp.max(carries[h]), carries[h]))
    o_ref[0] = acc_s[...].astype(BF16)


def _sb_sample(q, k, v, past_kt, past_vt, *, tk):
    bsz, tq, _ = q.shape
    past = past_kt.shape[3]
    grid = (bsz, SB_HEADS // 2)
    qspec = pl.BlockSpec((1, tq, LANES), lambda b, h: (b, 0, h))
    pspec = pl.BlockSpec((1, 2, SB_HEAD_DIM, past), lambda b, h: (b, h, 0, 0))
    return pl.pallas_call(
        functools.partial(_sb_sample_kernel, tq=tq, tk=tk, npast=past // tk),
        grid=grid,
        in_specs=[qspec, qspec, qspec, pspec, pspec],
        out_specs=qspec,
        out_shape=jax.ShapeDtypeStruct((bsz, tq, SB_WIDTH), BF16),
        scratch_shapes=[pltpu.VMEM((tq, LANES), F32)],
        compiler_params=pltpu.CompilerParams(
            dimension_semantics=("parallel", "parallel"), vmem_limit_bytes=VMEM_LIMIT),
        name="sb_sample",
    )(q, k, v, past_kt, past_vt)


def _post_kernel(x_ref, y_ref, o_ref, ga_ref, gb_ref, p_ref, wa_ref, wb_ref, wo_ref, gf_ref, wg_ref, wu_ref,
                 wd_ref, gp_ref, wpg_ref, wp_ref, gl_ref, out_ref, *, final_norm):
    branch_a = _dot(y_ref[...], wa_ref[...])
    branch_b = _dot(o_ref[...], wb_ref[...])
    merged = jax.nn.sigmoid(ga_ref[...]) * branch_a + jax.nn.sigmoid(gb_ref[...]) * branch_b
    x = x_ref[...] + _dot(merged.astype(BF16), wo_ref[...])
    hf = _rms(x, gf_ref[...]).astype(BF16)
    gate = _dot(hf, wg_ref[...])
    act = (gate * jax.nn.sigmoid(gate) * _dot(hf, wu_ref[...])).astype(BF16)
    x = x + _dot(act, wd_ref[...])
    g = jax.nn.sigmoid(_dot(_rms(x, gp_ref[...]).astype(BF16), wpg_ref[...]))
    x = x + g * _dot(p_ref[...].astype(BF16), wp_ref[...])
    if final_norm:
        x = _rms(x, gl_ref[...])
    out_ref[...] = x


def _post(x, y, o, ga, gb, p, wts, *, tm, final_norm):
    T = x.shape[0]

    def tok(n):
        return pl.BlockSpec((tm, n), lambda i: (i, 0))

    wspecs = [_const_spec(w.shape) for w in wts]
    return pl.pallas_call(
        functools.partial(_post_kernel, final_norm=final_norm),
        grid=(T // tm,),
        in_specs=[tok(D_MODEL), tok(M_INNER), tok(SB_WIDTH), tok(D_MODEL), tok(D_MODEL), tok(PLE_DIM)] + wspecs,
        out_specs=tok(D_MODEL),
        out_shape=jax.ShapeDtypeStruct((T, D_MODEL), F32),
        compiler_params=pltpu.CompilerParams(
            dimension_semantics=("parallel",), vmem_limit_bytes=VMEM_LIMIT),
        name="post",
    )(x, y, o, ga, gb, p, *wts)


def _pack_w_in(w_in):
    dt_cols = jnp.pad(w_in[:, OFF_DT:OFF_Q], ((0, 0), (0, DT_PAD - M_HEADS)))
    parts = [w_in[:, OFF_Z:OFF_DT], dt_cols, w_in[:, OFF_Q:OFF_K] * SB_SCALE, w_in[:, OFF_K:IN_COLS]]
    return jnp.concatenate(parts, axis=1).astype(BF16)


def _head_expand():
    r = lax.broadcasted_iota(jnp.int32, (DT_PAD, M_INNER), 0)
    c = lax.broadcasted_iota(jnp.int32, (DT_PAD, M_INNER), 1)
    return (c // M_HEAD_DIM == r).astype(BF16)


def _pad_heads(v):
    return jnp.pad(v.reshape(1, M_HEADS).astype(F32), ((0, 0), (0, DT_PAD - M_HEADS)))


def _layer(x, p, past_k, past_v, conv_hist, h0, prm, *, final_norm):
    bsz, L, _ = x.shape
    prompt = past_k is None
    if prompt:
        nb, tl = 1, min(L, TOKEN_TILE)
        conv_hist = jnp.zeros((bsz, CONV_W - 1, CONV_DIM), F32)
        h0 = jnp.zeros((bsz, M_HEADS, M_HEAD_DIM, M_STATE), F32)
    else:
        nb, tl = bsz, L
    zs, xs, bc, dt, q, k, v, new_k, new_v, ga, gb, new_conv = _inproj(
        x, conv_hist.astype(F32), prm["norm_mix_g"], prm["w_in"], prm["conv_w"], prm["conv_b"],
        nb=nb, tl=tl, kv_feature_major=prompt)
    if prompt:
        new_k, new_v = jnp.swapaxes(new_k, 2, 3), jnp.swapaxes(new_v, 2, 3)

    y, new_ssm = _ssd(xs, bc, zs, dt, h0.astype(F32), prm["dt_bias"], prm["a_log"], prm["d_skip"], prm["mnorm_g"],
                      prm["head_expand"], q=min(L, SSD_CHUNK))
    if prompt:
        o = _sb_prompt(q, k, v, tb=min(L, SB_BLOCK))
    else:
        o = _sb_sample(q, k, v, jnp.swapaxes(past_k.astype(F32), 2, 3), jnp.swapaxes(past_v.astype(F32), 2, 3),
                       tk=min(past_k.shape[2], SB_BLOCK))

    T = bsz * L
    flat = lambda a: a.reshape(T, a.shape[-1])
    x = _post(flat(x), flat(y), flat(o), flat(ga), flat(gb), flat(p), prm["post"], tm=min(T, TOKEN_TILE),
              final_norm=final_norm)
    return x.reshape(bsz, L, D_MODEL), new_k, new_v, new_conv, new_ssm


def _layer_params(i, norm_mix_g, w_in, conv_w, conv_b, dt_bias, a_log, d_skip, mnorm_g, w_a, w_b, w_out,
                  norm_ffn_g, w_gate, w_up, w_down, norm_ple_g, w_ple_gate, w_ple, final_norm_g):
    row = lambda v: v.reshape(1, -1).astype(F32)
    bf = lambda w: w.astype(BF16)
    return {
        "norm_mix_g": row(norm_mix_g[i]),
        "w_in": _pack_w_in(w_in[i]),
        "conv_w": conv_w[i].astype(F32),
        "conv_b": row(conv_b[i]),
        "dt_bias": _pad_heads(dt_bias[i]),
        "a_log": _pad_heads(a_log[i]),
        "d_skip": row(jnp.repeat(d_skip[i], M_HEAD_DIM)),
        "mnorm_g": row(mnorm_g[i]),
        "head_expand": _head_expand(),
        "post": (bf(w_a[i]), bf(w_b[i]), bf(w_out[i]), row(norm_ffn_g[i]), bf(w_gate[i]), bf(w_up[i]),
                 bf(w_down[i]), row(norm_ple_g[i]), bf(w_ple_gate[i]), bf(w_ple[i]), row(final_norm_g)),
    }


def kernel(x_prompt, x_sample, cache_k, cache_v, state_conv, state_ssm, p_prompt, p_sample, norm_mix_g, w_in,
           conv_w, conv_b, dt_bias, a_log, d_skip, mnorm_g, w_a, w_b, w_out, norm_ffn_g, w_gate, w_up, w_down,
           norm_ple_g, w_ple_gate, w_ple, final_norm_g):
    depth = w_in.shape[0]
    xp, xs = x_prompt, x_sample
    outs_p, outs_s = [], []
    for i in range(depth):
        prm = _layer_params(i, norm_mix_g, w_in, conv_w, conv_b, dt_bias, a_log, d_skip, mnorm_g, w_a, w_b,
                            w_out, norm_ffn_g, w_gate, w_up, w_down, norm_ple_g, w_ple_gate, w_ple, final_norm_g)
        last = i == depth - 1
        xp, *st_p = _layer(xp, p_prompt[i], None, None, None, None, prm, final_norm=last)
        xs, *st_s = _layer(xs, p_sample[i], cache_k[i], cache_v[i], state_conv[i], state_ssm[i], prm,
                           final_norm=last)
        outs_p.append(st_p)
        outs_s.append(st_s)
    stack = lambda outs, j: jnp.stack([o[j] for o in outs])
    return (xp, xs,
            stack(outs_p, 0), stack(outs_p, 1), stack(outs_p, 2), stack(outs_p, 3),
            stack(outs_s, 0), stack(outs_s, 1), stack(outs_s, 2), stack(outs_s, 3))
```

```python
import functools

import jax
import jax.numpy as jnp
from jax import lax
from jax.experimental import pallas as pl
from jax.experimental.pallas import tpu as pltpu

F32 = jnp.float32
BF16 = jnp.bfloat16

EPS = 1e-6
D_MODEL = 1024
M_HEAD_DIM = 64
M_HEADS = 16
M_GROUPS = 4
M_STATE = 128
M_INNER = M_HEADS * M_HEAD_DIM
CONV_W = 4
CONV_DIM = M_INNER + 2 * M_GROUPS * M_STATE
SB_HEADS = 16
SB_HEAD_DIM = 64
SB_WIDTH = SB_HEADS * SB_HEAD_DIM
SB_SCALE = SB_HEAD_DIM ** -0.5
D_FF = 2816
PLE_DIM = 256
OFF_Z = 0
OFF_XBC = OFF_Z + M_INNER
OFF_DT = OFF_XBC + CONV_DIM
OFF_Q = OFF_DT + M_HEADS
OFF_K = OFF_Q + SB_WIDTH
OFF_V = OFF_K + SB_WIDTH
OFF_GA = OFF_V + SB_WIDTH
OFF_GB = OFF_GA + D_MODEL
IN_COLS = OFF_GB + D_MODEL

LANES = 128
DT_PAD = LANES
P_Z = 0
P_XBC = P_Z + M_INNER
P_DT = P_XBC + CONV_DIM
P_Q = P_DT + DT_PAD
P_K = P_Q + SB_WIDTH
P_V = P_K + SB_WIDTH
P_GA = P_V + SB_WIDTH
P_GB = P_GA + D_MODEL
P_COLS = P_GB + D_MODEL

VMEM_LIMIT = 56 * 1024 * 1024
LOG2_E = 1.4426950408889634
SB_LOG_CUTOFF = -150.0
TOKEN_TILE = 256
SB_BLOCK = 256
SSD_CHUNK = 256


def _const_spec(shape):
    nd = len(shape)
    return pl.BlockSpec(shape, lambda *_: (0,) * nd, pipeline_mode=pl.Buffered(1))


def _rms(x, g):
    ms = jnp.mean(x * x, axis=-1, keepdims=True)
    return x * lax.rsqrt(ms + EPS) * g


def _split2(x):
    hi = x.astype(BF16)
    lo = (x - hi.astype(F32)).astype(BF16)
    return hi, lo


def _split3(x):
    hi = x.astype(BF16)
    r = x - hi.astype(F32)
    mid = r.astype(BF16)
    lo = (r - mid.astype(F32)).astype(BF16)
    return hi, mid, lo


def _dot(a, b):
    return jnp.dot(a, b, preferred_element_type=F32)


def _dot_nt(a, b):
    return lax.dot_general(a, b, (((1,), (1,)), ((), ())), preferred_element_type=F32)


def _dot_tn(a, b):
    return lax.dot_general(a, b, (((0,), (0,)), ((), ())), preferred_element_type=F32)


def _split_heads(x2, first):
    zero = jnp.zeros_like(x2)
    return jnp.where(first, x2, zero), jnp.where(first, zero, x2)


HIST_ROW = 8 - (CONV_W - 1)
BC_COLS = 2 * M_GROUPS * M_STATE
CONV_STRIP = 512


def _inproj_kernel(x_ref, hist_ref, g_ref, w_ref, cw_ref, cb_ref, zs_ref, xs_ref, bc_ref, dt_ref, q_ref, k_ref,
                   v_ref, ko_ref, vo_ref, ga_ref, gb_ref, nconv_ref, xp_s, *, nb, tl, kv_feature_major):
    t = pl.program_id(1)
    tm = nb * tl
    x = x_ref[...].reshape(tm, D_MODEL)
    h = _rms(x, g_ref[...]).astype(BF16)

    def proj(lo, n):
        return _dot(h, w_ref[:, lo:lo + n])

    @pl.when(t == 0)
    def _():
        xp_s[:, HIST_ROW:8, :] = hist_ref[...]

    def conv_cols(c0):
        cols = slice(c0, c0 + CONV_STRIP)
        xp_s[:, 8:8 + tl, cols] = proj(P_XBC + c0, CONV_STRIP).reshape(nb, tl, CONV_STRIP)
        xp = xp_s[:, :, cols]
        xc = cb_ref[:, cols] + xp[:, 8:8 + tl, :] * cw_ref[CONV_W - 1:CONV_W, cols]
        for k in range(1, CONV_W):
            xc = xc + pltpu.roll(xp, k, axis=1)[:, 8:8 + tl, :] * cw_ref[CONV_W - 1 - k:CONV_W - k, cols]
        xc = xc * jax.nn.sigmoid(xc)
        if c0 < M_INNER:
            xs_ref[:, :, cols] = xc
        else:
            bc_ref[:, :, c0 - M_INNER:c0 - M_INNER + CONV_STRIP] = xc.astype(BF16)
        tail = xp_s[:, 8 + tl - (CONV_W - 1):8 + tl, cols]
        nconv_ref[:, :, cols] = tail
        xp_s[:, HIST_ROW:8, cols] = tail

    def kv_proj(src, bf_ref, out_ref):
        kv = proj(src, SB_WIDTH)
        bf_ref[...] = kv.astype(BF16).reshape(nb, tl, SB_WIDTH)
        if kv_feature_major:
            out_ref[0] = kv.T.reshape(SB_HEADS, SB_HEAD_DIM, tl)
        else:
            for b in range(nb):
                for hd in range(SB_HEADS):
                    out_ref[b, hd] = kv[b * tl:(b + 1) * tl, hd * SB_HEAD_DIM:(hd + 1) * SB_HEAD_DIM]

    def z_proj():
        z = proj(P_Z, M_INNER)
        zs_ref[...] = (z * jax.nn.sigmoid(z)).reshape(nb, tl, M_INNER)

    def q_proj():
        dt_ref[...] = proj(P_DT, DT_PAD).reshape(nb, tl, DT_PAD)
        q_ref[...] = proj(P_Q, SB_WIDTH).astype(BF16).reshape(nb, tl, SB_WIDTH)

    def gate_proj(src, out_ref):
        out_ref[...] = proj(src, D_MODEL).reshape(nb, tl, D_MODEL)

    others = [z_proj, q_proj, functools.partial(kv_proj, P_K, k_ref, ko_ref),
              functools.partial(kv_proj, P_V, v_ref, vo_ref), functools.partial(gate_proj, P_GA, ga_ref),
              functools.partial(gate_proj, P_GB, gb_ref)]
    strips = list(range(0, CONV_DIM, CONV_STRIP))
    for i in range(max(len(others), len(strips))):
        if i < len(strips):
            conv_cols(strips[i])
        if i < len(others):
            others[i]()


def _inproj(x, hist, g, w, cw, cb, *, nb, tl, kv_feature_major):
    bsz, L, _ = x.shape
    grid = (bsz // nb, L // tl)

    def tok(n):
        return pl.BlockSpec((nb, tl, n), lambda b, t: (b, t, 0))

    if kv_feature_major:
        assert nb == 1
        head_shape = (bsz, SB_HEADS, SB_HEAD_DIM, L)
        head = pl.BlockSpec((1, SB_HEADS, SB_HEAD_DIM, tl), lambda b, t: (b, 0, 0, t))
    else:
        head_shape = (bsz, SB_HEADS, L, SB_HEAD_DIM)
        head = pl.BlockSpec((nb, SB_HEADS, tl, SB_HEAD_DIM), lambda b, t: (b, 0, t, 0))
    hist_spec = pl.BlockSpec((nb, CONV_W - 1, CONV_DIM), lambda b, t: (b, 0, 0))
    out_shape = (
        jax.ShapeDtypeStruct((bsz, L, M_INNER), F32),
        jax.ShapeDtypeStruct((bsz, L, M_INNER), F32),
        jax.ShapeDtypeStruct((bsz, L, BC_COLS), BF16),
        jax.ShapeDtypeStruct((bsz, L, DT_PAD), F32),
        jax.ShapeDtypeStruct((bsz, L, SB_WIDTH), BF16),
        jax.ShapeDtypeStruct((bsz, L, SB_WIDTH), BF16),
        jax.ShapeDtypeStruct((bsz, L, SB_WIDTH), BF16),
        jax.ShapeDtypeStruct(head_shape, F32),
        jax.ShapeDtypeStruct(head_shape, F32),
        jax.ShapeDtypeStruct((bsz, L, D_MODEL), F32),
        jax.ShapeDtypeStruct((bsz, L, D_MODEL), F32),
        jax.ShapeDtypeStruct((bsz, CONV_W - 1, CONV_DIM), F32),
    )
    out_specs = (tok(M_INNER), tok(M_INNER), tok(BC_COLS), tok(DT_PAD), tok(SB_WIDTH), tok(SB_WIDTH), tok(SB_WIDTH),
                 head, head, tok(D_MODEL), tok(D_MODEL), hist_spec)
    return pl.pallas_call(
        functools.partial(_inproj_kernel, nb=nb, tl=tl, kv_feature_major=kv_feature_major),
        grid=grid,
        in_specs=[tok(D_MODEL), hist_spec, _const_spec((1, D_MODEL)), _const_spec((D_MODEL, P_COLS)),
                  _const_spec((CONV_W, CONV_DIM)), _const_spec((1, CONV_DIM))],
        out_specs=out_specs,
        out_shape=out_shape,
        scratch_shapes=[pltpu.VMEM((nb, 8 + tl, CONV_DIM), F32)],
        compiler_params=pltpu.CompilerParams(
            dimension_semantics=("parallel", "arbitrary"), vmem_limit_bytes=VMEM_LIMIT),
        name="inproj",
    )(x, hist, g, w, cw, cb)


PAIRS = M_HEADS // 2
HEADS_PER_GROUP = M_HEADS // M_GROUPS
GROUP_COLS = HEADS_PER_GROUP * M_HEAD_DIM


def _ssd_kernel(xs_ref, bc_ref, zs_ref, dt_ref, h0_ref, dtb_ref, alog_ref, dsk_ref, mg_ref, ex_ref,
                y_ref, nssm_ref, h_s, *, q):
    t = pl.program_id(1)
    nt = pl.num_programs(1)

    @pl.when(t == 0)
    def _():
        for i in range(PAIRS):
            h_s[:, i * LANES:(i + 1) * LANES] = h0_ref[0, i].T

    xs = xs_ref[0]
    bm = bc_ref[0, :, 0:M_GROUPS * M_STATE]
    cm = bc_ref[0, :, M_GROUPS * M_STATE:BC_COLS]
    xdt = dt_ref[0] + dtb_ref[...]
    dt = jnp.maximum(xdt, 0.0) + jnp.log1p(jnp.exp(-jnp.abs(xdt)))
    da = dt * (-jnp.exp(alog_ref[...]))

    ri = lax.broadcasted_iota(jnp.int32, (q, q), 0)
    ci = lax.broadcasted_iota(jnp.int32, (q, q), 1)
    causal = ri >= ci
    tri = causal.astype(BF16)
    first = lax.broadcasted_iota(jnp.int32, (q, LANES), 1) < M_HEAD_DIM
    ex = ex_ref[...]

    def head_rows(a):
        if q % LANES:
            a = jnp.concatenate([a, jnp.zeros((LANES - q, LANES), F32)], axis=0)
        return a.T[:, :q]

    acum = sum(_dot(tri, p) for p in _split3(da))
    tot = acum[q - 1:q, :]
    ey = jnp.exp(acum)
    ws = jnp.exp(tot - acum) * dt
    dc = jnp.broadcast_to(jnp.exp(tot), (8, DT_PAD))
    stack_c = _dot(jnp.concatenate([ey, ws], axis=0).astype(BF16), ex)
    ey_c = stack_c[0:q]
    ws_c = stack_c[q:2 * q]
    dc_c = sum(_dot(p, ex) for p in _split2(dc))[0:1]
    acum_t = head_rows(acum)
    dt_t = head_rows(dt)

    xs_bf = xs.astype(BF16)
    xw_bf = (xs * ws_c).astype(BF16)
    h_prev = h_s[...]
    h_bf = h_prev.astype(BF16)

    ydiag, yoff, states = [], [], []
    for g in range(M_GROUPS):
        bg = bm[:, g * M_STATE:(g + 1) * M_STATE]
        cg = cm[:, g * M_STATE:(g + 1) * M_STATE]
        gcols = slice(g * GROUP_COLS, (g + 1) * GROUP_COLS)
        cb = _dot_nt(cg, bg)
        yoff.append(_dot(cg, h_bf[:, gcols]))
        states.append(_dot_tn(bg, xw_bf[:, gcols]))
        for pr in range(HEADS_PER_GROUP // 2):
            ms = []
            for r in range(2):
                hd = g * HEADS_PER_GROUP + pr * 2 + r
                acol = jnp.broadcast_to(acum[:, hd:hd + 1], (q, q))
                arow = jnp.broadcast_to(acum_t[hd:hd + 1, :], (q, q))
                dec = jnp.exp(jnp.where(causal, acol - arow, -1e30))
                ms.append((cb * dec * jnp.broadcast_to(dt_t[hd:hd + 1, :], (q, q))).astype(BF16))
            pair = g * (HEADS_PER_GROUP // 2) + pr
            xpair = xs_bf[:, pair * LANES:(pair + 1) * LANES]
            if q % LANES == 0:
                ydiag.append(_dot(jnp.concatenate(ms, axis=1), jnp.concatenate(_split_heads(xpair, first), axis=0)))
            else:
                ydiag.append(jnp.where(first, _dot(ms[0], xpair), _dot(ms[1], xpair)))
    y = jnp.concatenate(ydiag, axis=1) + jnp.concatenate(yoff, axis=1) * ey_c + dsk_ref[...] * xs
    y = y * zs_ref[0]
    normed = []
    for g in range(M_GROUPS):
        yg = y[:, g * GROUP_COLS:(g + 1) * GROUP_COLS]
        normed.append(yg * lax.rsqrt(jnp.mean(yg * yg, axis=-1, keepdims=True) + EPS))
    y_ref[0] = (jnp.concatenate(normed, axis=1) * mg_ref[...]).astype(BF16)
    h_s[...] = h_prev * dc_c + jnp.concatenate(states, axis=1)

    @pl.when(t == nt - 1)
    def _():
        for i in range(PAIRS):
            nssm_ref[0, i] = h_s[:, i * LANES:(i + 1) * LANES].T


def _ssd(xs, bc, zs, dt, h0, dtb, alog, dsk, mg, ex, *, q):
    bsz, L, _ = xs.shape
    grid = (bsz, L // q)
    h0p = h0.reshape(bsz, PAIRS, LANES, M_STATE)

    def tok(n):
        return pl.BlockSpec((1, q, n), lambda b, t: (b, t, 0))

    per_b = pl.BlockSpec((1, PAIRS, LANES, M_STATE), lambda b, t: (b, 0, 0, 0))
    y, nssm = pl.pallas_call(
        functools.partial(_ssd_kernel, q=q),
        grid=grid,
        in_specs=[tok(M_INNER), tok(BC_COLS), tok(M_INNER), tok(DT_PAD), per_b,
                  _const_spec((1, DT_PAD)), _const_spec((1, DT_PAD)), _const_spec((1, M_INNER)),
                  _const_spec((1, M_INNER)), _const_spec((DT_PAD, M_INNER))],
        out_specs=(tok(M_INNER), per_b),
        out_shape=(jax.ShapeDtypeStruct((bsz, L, M_INNER), BF16),
                   jax.ShapeDtypeStruct((bsz, PAIRS, LANES, M_STATE), F32)),
        scratch_shapes=[pltpu.VMEM((M_STATE, M_INNER), F32)],
        compiler_params=pltpu.CompilerParams(
            dimension_semantics=("parallel", "arbitrary"), vmem_limit_bytes=VMEM_LIMIT),
        name="ssd",
    )(xs, bc, zs, dt, h0p, dtb, alog, dsk, mg, ex)
    return y, nssm.reshape(bsz, M_HEADS, M_HEAD_DIM, M_STATE)


def _sb_weights(z, tri, carry, mask):
    u = z * LOG2_E
    neg_abs = lax.bitcast_convert_type(lax.bitcast_convert_type(u, jnp.uint32) | jnp.uint32(0x80000000), F32)
    drop = jnp.maximum(u, 0.0) + jnp.log2(1.0 + jnp.exp2(neg_abs))
    if mask is not None:
        drop = jnp.where(mask, drop, 0.0)
    cs = _dot(drop.astype(BF16), tri)
    w = jnp.exp2(u - cs + carry)
    if mask is not None:
        w = jnp.where(mask, w, 0.0)
    return w.astype(BF16), carry - cs[:, 0:1]


def _suffix_tri(n):
    r = lax.broadcasted_iota(jnp.int32, (n, n), 0)
    c = lax.broadcasted_iota(jnp.int32, (n, n), 1)
    return (r >= c).astype(BF16), c < r


SB_STEP_HEADS = 4


def _sb_prompt_kernel(q_ref, k_ref, v_ref, o_ref, acc_s, c_s, *, tb):
    qi = pl.program_id(2)
    nh = SB_STEP_HEADS
    width = nh * SB_HEAD_DIM
    first = lax.broadcasted_iota(jnp.int32, (tb, LANES), 1) < SB_HEAD_DIM
    lane_head = lax.broadcasted_iota(jnp.int32, (tb, width), 1) // SB_HEAD_DIM
    tri, diag_mask = _suffix_tri(tb)
    qh = []
    for p in range(nh // 2):
        qh += list(_split_heads(q_ref[0, :, p * LANES:(p + 1) * LANES], first))

    def weights(j, mask, carries):
        rows = pl.ds(pl.multiple_of(j * tb, tb), tb)
        ws, cs = [], []
        for i in range(nh):
            k2 = k_ref[0, rows, (i // 2) * LANES:(i // 2 + 1) * LANES]
            w, c = _sb_weights(_dot_nt(qh[i], k2), tri, carries[i], mask)
            ws.append(w)
            cs.append(c)
        return ws, cs

    def values(j):
        v = v_ref[0, pl.ds(pl.multiple_of(j * tb, tb), tb), :]
        zero = jnp.zeros_like(v)
        return [jnp.where(lane_head == i, v, zero) for i in range(nh)]

    def pv(ws, vs):
        return _dot(jnp.concatenate(ws, axis=1), jnp.concatenate(vs, axis=0))

    def keep(cs):
        for i in range(nh):
            c_s[i] = cs[i]

    zero = [jnp.zeros((tb, 1), F32)] * nh

    @pl.when(qi == 0)
    def _():
        ws, cs = weights(0, diag_mask, zero)
        acc_s[...] = pv(ws, values(0))
        keep(cs)

    @pl.when(qi > 0)
    def _():
        ws_d, cs = weights(qi, diag_mask, zero)
        ws_p, cs = weights(qi - 1, None, cs)
        acc_s[...] = pv(ws_d + ws_p, values(qi) + values(qi - 1))
        keep(cs)

    def live():
        return functools.reduce(jnp.maximum, [jnp.max(c_s[i]) for i in range(nh)])

    def cond(st):
        j, m = st
        return jnp.logical_and(j >= 0, m > SB_LOG_CUTOFF)

    def body(st):
        j, _ = st
        ws, cs = weights(j, None, [c_s[i] for i in range(nh)])
        acc_s[...] += pv(ws, values(j))
        keep(cs)
        return j - 1, live()

    lax.while_loop(cond, body, (qi - 2, live()))
    o_ref[0] = acc_s[...].astype(BF16)


def _sb_prompt(q, k, v, *, tb):
    bsz, L, _ = q.shape
    width = SB_STEP_HEADS * SB_HEAD_DIM
    grid = (bsz, SB_HEADS // SB_STEP_HEADS, L // tb)
    qspec = pl.BlockSpec((1, tb, width), lambda b, h, i: (b, i, h))
    kspec = pl.BlockSpec((1, L, width), lambda b, h, i: (b, 0, h))
    return pl.pallas_call(
        functools.partial(_sb_prompt_kernel, tb=tb),
        grid=grid,
        in_specs=[qspec, kspec, kspec],
        out_specs=qspec,
        out_shape=jax.ShapeDtypeStruct((bsz, L, SB_WIDTH), BF16),
        scratch_shapes=[pltpu.VMEM((tb, width), F32), pltpu.VMEM((SB_STEP_HEADS, tb, 1), F32)],
        compiler_params=pltpu.CompilerParams(
            dimension_semantics=("parallel", "parallel", "arbitrary"), vmem_limit_bytes=VMEM_LIMIT),
        name="sb_prompt",
    )(q, k, v)


def _sb_sample_kernel(q_ref, k_ref, v_ref, pk_ref, pv_ref, o_ref, acc_s, c_s, *, tq, tk, npast):
    nh = SB_STEP_HEADS
    width = nh * SB_HEAD_DIM
    q2 = q_ref[0]
    first = lax.broadcasted_iota(jnp.int32, (tq, LANES), 1) < SB_HEAD_DIM
    lane_head = lax.broadcasted_iota(jnp.int32, (tq, width), 1) // SB_HEAD_DIM
    row_head = lax.broadcasted_iota(jnp.int32, (width, tk), 0) // SB_HEAD_DIM
    tri_d, diag_mask = _suffix_tri(tq)
    tri_p, _ = _suffix_tri(tk)
    qh = []
    for p in range(nh // 2):
        qh += list(_split_heads(q2[:, p * LANES:(p + 1) * LANES], first))

    def cache_block(j, carries):
        cols = pl.ds(pl.multiple_of(j * tk, tk), tk)
        ws, cs = [], []
        for i in range(nh):
            qs = q2[:, i * SB_HEAD_DIM:(i + 1) * SB_HEAD_DIM]
            w, c = _sb_weights(_dot(qs, pk_ref[0, i, :, cols].astype(BF16)), tri_p, carries[i], None)
            ws.append(w)
            cs.append(c)
        vt = pv_ref[0, :, :, cols].astype(BF16).reshape(width, tk)
        zero = jnp.zeros_like(vt)
        vts = jnp.concatenate([jnp.where(row_head == i, vt, zero) for i in range(nh)], axis=1)
        return jnp.concatenate(ws, axis=1), cs, vts

    ws_d, cs = [], []
    for i in range(nh):
        k2 = k_ref[0, :, (i // 2) * LANES:(i // 2 + 1) * LANES]
        w, c = _sb_weights(_dot_nt(qh[i], k2), tri_d, jnp.zeros((tq, 1), F32), diag_mask)
        ws_d.append(w)
        cs.append(c)
    vnew = v_ref[0]
    vals_d = jnp.concatenate([jnp.where(lane_head == i, vnew, jnp.zeros_like(vnew)) for i in range(nh)], axis=0)
    ws_p, cs, vts = cache_block(npast - 1, cs)
    acc_s[...] = _dot(jnp.concatenate(ws_d, axis=1), vals_d) + _dot_nt(ws_p, vts)
    for i in range(nh):
        c_s[i] = cs[i]

    def live():
        return functools.reduce(jnp.maximum, [jnp.max(c_s[i]) for i in range(nh)])

    def cond(st):
        j, m = st
        return jnp.logical_and(j >= 0, m > SB_LOG_CUTOFF)

    def body(st):
        j, _ = st
        ws, cs, vts = cache_block(j, [c_s[i] for i in range(nh)])
        acc_s[...] += _dot_nt(ws, vts)
        for i in range(nh):
            c_s[i] = cs[i]
        return j - 1, live()

    lax.while_loop(cond, body, (jnp.int32(npast - 2), live()))
    o_ref[0] = acc_s[...].astype(BF16)


def _sb_sample(q, k, v, past_kt, past_vt, *, tk):
    bsz, tq, _ = q.shape
    past = past_kt.shape[3]
    width = SB_STEP_HEADS * SB_HEAD_DIM
    grid = (bsz, SB_HEADS // SB_STEP_HEADS)
    qspec = pl.BlockSpec((1, tq, width), lambda b, h: (b, 0, h))
    pspec = pl.BlockSpec((1, SB_STEP_HEADS, SB_HEAD_DIM, past), lambda b, h: (b, h, 0, 0))
    return pl.pallas_call(
        functools.partial(_sb_sample_kernel, tq=tq, tk=tk, npast=past // tk),
        grid=grid,
        in_specs=[qspec, qspec, qspec, pspec, pspec],
        out_specs=qspec,
        out_shape=jax.ShapeDtypeStruct((bsz, tq, SB_WIDTH), BF16),
        scratch_shapes=[pltpu.VMEM((tq, width), F32), pltpu.VMEM((SB_STEP_HEADS, tq, 1), F32)],
        compiler_params=pltpu.CompilerParams(
            dimension_semantics=("parallel", "parallel"), vmem_limit_bytes=VMEM_LIMIT),
        name="sb_sample",
    )(q, k, v, past_kt, past_vt)


def _post_kernel(x_ref, y_ref, o_ref, ga_ref, gb_ref, p_ref, wa_ref, wb_ref, wo_ref, gf_ref, wg_ref, wu_ref,
                 wd_ref, gp_ref, wpg_ref, wp_ref, gl_ref, out_ref, *, final_norm):
    branch_a = _dot(y_ref[...], wa_ref[...])
    branch_b = _dot(o_ref[...], wb_ref[...])
    merged = jax.nn.sigmoid(ga_ref[...]) * branch_a + jax.nn.sigmoid(gb_ref[...]) * branch_b
    x = x_ref[...] + _dot(merged.astype(BF16), wo_ref[...])
    hf = _rms(x, gf_ref[...]).astype(BF16)
    gate = _dot(hf, wg_ref[...])
    act = (gate * jax.nn.sigmoid(gate) * _dot(hf, wu_ref[...])).astype(BF16)
    x = x + _dot(act, wd_ref[...])
    g = jax.nn.sigmoid(_dot(_rms(x, gp_ref[...]).astype(BF16), wpg_ref[...]))
    x = x + g * _dot(p_ref[...].astype(BF16), wp_ref[...])
    if final_norm:
        x = _rms(x, gl_ref[...])
    out_ref[...] = x


def _post(x, y, o, ga, gb, p, wts, *, tm, final_norm):
    T = x.shape[0]

    def tok(n):
        return pl.BlockSpec((tm, n), lambda i: (i, 0))

    wspecs = [_const_spec(w.shape) for w in wts]
    return pl.pallas_call(
        functools.partial(_post_kernel, final_norm=final_norm),
        grid=(T // tm,),
        in_specs=[tok(D_MODEL), tok(M_INNER), tok(SB_WIDTH), tok(D_MODEL), tok(D_MODEL), tok(PLE_DIM)] + wspecs,
        out_specs=tok(D_MODEL),
        out_shape=jax.ShapeDtypeStruct((T, D_MODEL), F32),
        compiler_params=pltpu.CompilerParams(
            dimension_semantics=("parallel",), vmem_limit_bytes=VMEM_LIMIT),
        name="post",
    )(x, y, o, ga, gb, p, *wts)


def _pack_w_in(w_in):
    dt_cols = jnp.pad(w_in[:, OFF_DT:OFF_Q], ((0, 0), (0, DT_PAD - M_HEADS)))
    parts = [w_in[:, OFF_Z:OFF_DT], dt_cols, w_in[:, OFF_Q:OFF_K] * SB_SCALE, w_in[:, OFF_K:IN_COLS]]
    return jnp.concatenate(parts, axis=1).astype(BF16)


def _head_expand():
    r = lax.broadcasted_iota(jnp.int32, (DT_PAD, M_INNER), 0)
    c = lax.broadcasted_iota(jnp.int32, (DT_PAD, M_INNER), 1)
    return (c // M_HEAD_DIM == r).astype(BF16)


def _pad_heads(v):
    return jnp.pad(v.reshape(1, M_HEADS).astype(F32), ((0, 0), (0, DT_PAD - M_HEADS)))


def _layer(x, p, past_k, past_v, conv_hist, h0, prm, *, final_norm):
    bsz, L, _ = x.shape
    prompt = past_k is None
    if prompt:
        nb, tl = 1, min(L, TOKEN_TILE)
        conv_hist = jnp.zeros((bsz, CONV_W - 1, CONV_DIM), F32)
        h0 = jnp.zeros((bsz, M_HEADS, M_HEAD_DIM, M_STATE), F32)
    else:
        nb, tl = bsz, L
    zs, xs, bc, dt, q, k, v, new_k, new_v, ga, gb, new_conv = _inproj(
        x, conv_hist.astype(F32), prm["norm_mix_g"], prm["w_in"], prm["conv_w"], prm["conv_b"],
        nb=nb, tl=tl, kv_feature_major=prompt)
    if prompt:
        new_k, new_v = jnp.swapaxes(new_k, 2, 3), jnp.swapaxes(new_v, 2, 3)

    y, new_ssm = _ssd(xs, bc, zs, dt, h0.astype(F32), prm["dt_bias"], prm["a_log"], prm["d_skip"], prm["mnorm_g"],
                      prm["head_expand"], q=min(L, SSD_CHUNK))
    if prompt:
        o = _sb_prompt(q, k, v, tb=min(L, SB_BLOCK))
    else:
        o = _sb_sample(q, k, v, jnp.swapaxes(past_k.astype(F32), 2, 3), jnp.swapaxes(past_v.astype(F32), 2, 3),
                       tk=min(past_k.shape[2], SB_BLOCK))

    T = bsz * L
    flat = lambda a: a.reshape(T, a.shape[-1])
    x = _post(flat(x), flat(y), flat(o), flat(ga), flat(gb), flat(p), prm["post"], tm=min(T, TOKEN_TILE),
              final_norm=final_norm)
    return x.reshape(bsz, L, D_MODEL), new_k, new_v, new_conv, new_ssm


def _layer_params(i, norm_mix_g, w_in, conv_w, conv_b, dt_bias, a_log, d_skip, mnorm_g, w_a, w_b, w_out,
                  norm_ffn_g, w_gate, w_up, w_down, norm_ple_g, w_ple_gate, w_ple, final_norm_g):
    row = lambda v: v.reshape(1, -1).astype(F32)
    bf = lambda w: w.astype(BF16)
    return {
        "norm_mix_g": row(norm_mix_g[i]),
        "w_in": _pack_w_in(w_in[i]),
        "conv_w": conv_w[i].astype(F32),
        "conv_b": row(conv_b[i]),
        "dt_bias": _pad_heads(dt_bias[i]),
        "a_log": _pad_heads(a_log[i]),
        "d_skip": row(jnp.repeat(d_skip[i], M_HEAD_DIM)),
        "mnorm_g": row(mnorm_g[i]),
        "head_expand": _head_expand(),
        "post": (bf(w_a[i]), bf(w_b[i]), bf(w_out[i]), row(norm_ffn_g[i]), bf(w_gate[i]), bf(w_up[i]),
                 bf(w_down[i]), row(norm_ple_g[i]), bf(w_ple_gate[i]), bf(w_ple[i]), row(final_norm_g)),
    }


def kernel(x_prompt, x_sample, cache_k, cache_v, state_conv, state_ssm, p_prompt, p_sample, norm_mix_g, w_in,
           conv_w, conv_b, dt_bias, a_log, d_skip, mnorm_g, w_a, w_b, w_out, norm_ffn_g, w_gate, w_up, w_down,
           norm_ple_g, w_ple_gate, w_ple, final_norm_g):
    depth = w_in.shape[0]
    xp, xs = x_prompt, x_sample
    outs_p, outs_s = [], []
    for i in range(depth):
        prm = _layer_params(i, norm_mix_g, w_in, conv_w, conv_b, dt_bias, a_log, d_skip, mnorm_g, w_a, w_b,
                            w_out, norm_ffn_g, w_gate, w_up, w_down, norm_ple_g, w_ple_gate, w_ple, final_norm_g)
        last = i == depth - 1
        xp, *st_p = _layer(xp, p_prompt[i], None, None, None, None, prm, final_norm=last)
        xs, *st_s = _layer(xs, p_sample[i], cache_k[i], cache_v[i], state_conv[i], state_ssm[i], prm,
                           final_norm=last)
        outs_p.append(st_p)
        outs_s.append(st_s)
    stack = lambda outs, j: jnp.stack([o[j] for o in outs])
    return (xp, xs,
            stack(outs_p, 0), stack(outs_p, 1), stack(outs_p, 2), stack(outs_p, 3),
            stack(outs_s, 0), stack(outs_s, 1), stack(outs_s, 2), stack(outs_s, 3))
```

```python
import functools

import jax
import jax.numpy as jnp
from jax import lax
from jax.experimental import pallas as pl
from jax.experimental.pallas import tpu as pltpu

F32 = jnp.float32
BF16 = jnp.bfloat16

EPS = 1e-6
D_MODEL = 1024
M_HEAD_DIM = 64
M_HEADS = 16
M_GROUPS = 4
M_STATE = 128
M_INNER = M_HEADS * M_HEAD_DIM
CONV_W = 4
CONV_DIM = M_INNER + 2 * M_GROUPS * M_STATE
SB_HEADS = 16
SB_HEAD_DIM = 64
SB_WIDTH = SB_HEADS * SB_HEAD_DIM
SB_SCALE = SB_HEAD_DIM ** -0.5
D_FF = 2816
PLE_DIM = 256
OFF_Z = 0
OFF_XBC = OFF_Z + M_INNER
OFF_DT = OFF_XBC + CONV_DIM
OFF_Q = OFF_DT + M_HEADS
OFF_K = OFF_Q + SB_WIDTH
OFF_V = OFF_K + SB_WIDTH
OFF_GA = OFF_V + SB_WIDTH
OFF_GB = OFF_GA + D_MODEL
IN_COLS = OFF_GB + D_MODEL

LANES = 128
DT_PAD = LANES
P_Z = 0
P_XBC = P_Z + M_INNER
P_DT = P_XBC + CONV_DIM
P_Q = P_DT + DT_PAD
P_K = P_Q + SB_WIDTH
P_V = P_K + SB_WIDTH
P_GA = P_V + SB_WIDTH
P_GB = P_GA + D_MODEL
P_COLS = P_GB + D_MODEL

VMEM_LIMIT = 56 * 1024 * 1024
LOG2_E = 1.4426950408889634
SB_LOG_CUTOFF = -150.0
TOKEN_TILE = 256
SB_BLOCK = 256
SSD_CHUNK = 256


def _const_spec(shape):
    nd = len(shape)
    return pl.BlockSpec(shape, lambda *_: (0,) * nd, pipeline_mode=pl.Buffered(1))


def _rms(x, g):
    ms = jnp.mean(x * x, axis=-1, keepdims=True)
    return x * lax.rsqrt(ms + EPS) * g


def _split2(x):
    hi = x.astype(BF16)
    lo = (x - hi.astype(F32)).astype(BF16)
    return hi, lo


def _split3(x):
    hi = x.astype(BF16)
    r = x - hi.astype(F32)
    mid = r.astype(BF16)
    lo = (r - mid.astype(F32)).astype(BF16)
    return hi, mid, lo


def _dot(a, b):
    return jnp.dot(a, b, preferred_element_type=F32)


def _dot_nt(a, b):
    return lax.dot_general(a, b, (((1,), (1,)), ((), ())), preferred_element_type=F32)


def _dot_tn(a, b):
    return lax.dot_general(a, b, (((0,), (0,)), ((), ())), preferred_element_type=F32)


def _split_heads(x2, first):
    zero = jnp.zeros_like(x2)
    return jnp.where(first, x2, zero), jnp.where(first, zero, x2)


HIST_ROW = 8 - (CONV_W - 1)
BC_COLS = 2 * M_GROUPS * M_STATE
CONV_STRIP = 512


def _inproj_kernel(x_ref, hist_ref, g_ref, w_ref, cw_ref, cb_ref, zs_ref, xs_ref, bc_ref, dt_ref, q_ref, k_ref,
                   v_ref, ko_ref, vo_ref, ga_ref, gb_ref, nconv_ref, xp_s, *, nb, tl, kv_feature_major):
    t = pl.program_id(1)
    tm = nb * tl
    x = x_ref[...].reshape(tm, D_MODEL)
    h = _rms(x, g_ref[...]).astype(BF16)

    def proj(lo, n):
        return _dot(h, w_ref[:, lo:lo + n])

    @pl.when(t == 0)
    def _():
        xp_s[:, HIST_ROW:8, :] = hist_ref[...]

    def conv_cols(c0):
        cols = slice(c0, c0 + CONV_STRIP)
        xp_s[:, 8:8 + tl, cols] = proj(P_XBC + c0, CONV_STRIP).reshape(nb, tl, CONV_STRIP)
        xp = xp_s[:, :, cols]
        xc = cb_ref[:, cols] + xp[:, 8:8 + tl, :] * cw_ref[CONV_W - 1:CONV_W, cols]
        for k in range(1, CONV_W):
            xc = xc + pltpu.roll(xp, k, axis=1)[:, 8:8 + tl, :] * cw_ref[CONV_W - 1 - k:CONV_W - k, cols]
        xc = xc * jax.nn.sigmoid(xc)
        if c0 < M_INNER:
            xs_ref[:, :, cols] = xc
        else:
            bc_ref[:, :, c0 - M_INNER:c0 - M_INNER + CONV_STRIP] = xc.astype(BF16)
        tail = xp_s[:, 8 + tl - (CONV_W - 1):8 + tl, cols]
        nconv_ref[:, :, cols] = tail
        xp_s[:, HIST_ROW:8, cols] = tail

    def kv_proj(src, bf_ref, out_ref):
        kv = proj(src, SB_WIDTH)
        bf_ref[...] = kv.astype(BF16).reshape(nb, tl, SB_WIDTH)
        if kv_feature_major:
            out_ref[0] = kv.T.reshape(SB_HEADS, SB_HEAD_DIM, tl)
        else:
            for b in range(nb):
                for hd in range(SB_HEADS):
                    out_ref[b, hd] = kv[b * tl:(b + 1) * tl, hd * SB_HEAD_DIM:(hd + 1) * SB_HEAD_DIM]

    def z_proj():
        z = proj(P_Z, M_INNER)
        zs_ref[...] = (z * jax.nn.sigmoid(z)).reshape(nb, tl, M_INNER)

    def q_proj():
        dt_ref[...] = proj(P_DT, DT_PAD).reshape(nb, tl, DT_PAD)
        q_ref[...] = proj(P_Q, SB_WIDTH).astype(BF16).reshape(nb, tl, SB_WIDTH)

    def gate_proj(src, out_ref):
        out_ref[...] = proj(src, D_MODEL).reshape(nb, tl, D_MODEL)

    others = [z_proj, q_proj, functools.partial(kv_proj, P_K, k_ref, ko_ref),
              functools.partial(kv_proj, P_V, v_ref, vo_ref), functools.partial(gate_proj, P_GA, ga_ref),
              functools.partial(gate_proj, P_GB, gb_ref)]
    strips = list(range(0, CONV_DIM, CONV_STRIP))
    for i in range(max(len(others), len(strips))):
        if i < len(strips):
            conv_cols(strips[i])
        if i < len(others):
            others[i]()


def _inproj(x, hist, g, w, cw, cb, *, nb, tl, kv_feature_major):
    bsz, L, _ = x.shape
    grid = (bsz // nb, L // tl)

    def tok(n):
        return pl.BlockSpec((nb, tl, n), lambda b, t: (b, t, 0))

    if kv_feature_major:
        assert nb == 1
        head_shape = (bsz, SB_HEADS, SB_HEAD_DIM, L)
        head = pl.BlockSpec((1, SB_HEADS, SB_HEAD_DIM, tl), lambda b, t: (b, 0, 0, t))
    else:
        head_shape = (bsz, SB_HEADS, L, SB_HEAD_DIM)
        head = pl.BlockSpec((nb, SB_HEADS, tl, SB_HEAD_DIM), lambda b, t: (b, 0, t, 0))
    hist_spec = pl.BlockSpec((nb, CONV_W - 1, CONV_DIM), lambda b, t: (b, 0, 0))
    out_shape = (
        jax.ShapeDtypeStruct((bsz, L, M_INNER), F32),
        jax.ShapeDtypeStruct((bsz, L, M_INNER), F32),
        jax.ShapeDtypeStruct((bsz, L, BC_COLS), BF16),
        jax.ShapeDtypeStruct((bsz, L, DT_PAD), F32),
        jax.ShapeDtypeStruct((bsz, L, SB_WIDTH), BF16),
        jax.ShapeDtypeStruct((bsz, L, SB_WIDTH), BF16),
        jax.ShapeDtypeStruct((bsz, L, SB_WIDTH), BF16),
        jax.ShapeDtypeStruct(head_shape, F32),
        jax.ShapeDtypeStruct(head_shape, F32),
        jax.ShapeDtypeStruct((bsz, L, D_MODEL), F32),
        jax.ShapeDtypeStruct((bsz, L, D_MODEL), F32),
        jax.ShapeDtypeStruct((bsz, CONV_W - 1, CONV_DIM), F32),
    )
    out_specs = (tok(M_INNER), tok(M_INNER), tok(BC_COLS), tok(DT_PAD), tok(SB_WIDTH), tok(SB_WIDTH), tok(SB_WIDTH),
                 head, head, tok(D_MODEL), tok(D_MODEL), hist_spec)
    return pl.pallas_call(
        functools.partial(_inproj_kernel, nb=nb, tl=tl, kv_feature_major=kv_feature_major),
        grid=grid,
        in_specs=[tok(D_MODEL), hist_spec, _const_spec((1, D_MODEL)), _const_spec((D_MODEL, P_COLS)),
                  _const_spec((CONV_W, CONV_DIM)), _const_spec((1, CONV_DIM))],
        out_specs=out_specs,
        out_shape=out_shape,
        scratch_shapes=[pltpu.VMEM((nb, 8 + tl, CONV_DIM), F32)],
        compiler_params=pltpu.CompilerParams(
            dimension_semantics=("parallel", "arbitrary"), vmem_limit_bytes=VMEM_LIMIT),
        name="inproj",
    )(x, hist, g, w, cw, cb)


PAIRS = M_HEADS // 2
HEADS_PER_GROUP = M_HEADS // M_GROUPS
GROUP_COLS = HEADS_PER_GROUP * M_HEAD_DIM


def _ssd_kernel(xs_ref, bc_ref, zs_ref, dt_ref, h0_ref, dtb_ref, alog_ref, dsk_ref, mg_ref, ex_ref,
                y_ref, nssm_ref, h_s, *, q):
    t = pl.program_id(1)
    nt = pl.num_programs(1)

    @pl.when(t == 0)
    def _():
        for i in range(PAIRS):
            h_s[:, i * LANES:(i + 1) * LANES] = h0_ref[0, i].T

    xs = xs_ref[0]
    bm = bc_ref[0, :, 0:M_GROUPS * M_STATE]
    cm = bc_ref[0, :, M_GROUPS * M_STATE:BC_COLS]
    xdt = dt_ref[0] + dtb_ref[...]
    dt = jnp.maximum(xdt, 0.0) + jnp.log1p(jnp.exp(-jnp.abs(xdt)))
    da = dt * (-jnp.exp(alog_ref[...]))

    ri = lax.broadcasted_iota(jnp.int32, (q, q), 0)
    ci = lax.broadcasted_iota(jnp.int32, (q, q), 1)
    causal = ri >= ci
    tri = causal.astype(BF16)
    first = lax.broadcasted_iota(jnp.int32, (q, LANES), 1) < M_HEAD_DIM
    ex = ex_ref[...]

    def head_rows(a):
        if q % LANES:
            a = jnp.concatenate([a, jnp.zeros((LANES - q, LANES), F32)], axis=0)
        return a.T[:, :q]

    acum = sum(_dot(tri, p) for p in _split3(da))
    tot = acum[q - 1:q, :]
    ey = jnp.exp(acum)
    ws = jnp.exp(tot - acum) * dt
    dc = jnp.broadcast_to(jnp.exp(tot), (8, DT_PAD))
    stack_c = _dot(jnp.concatenate([ey, ws], axis=0).astype(BF16), ex)
    ey_c = stack_c[0:q]
    ws_c = stack_c[q:2 * q]
    dc_c = sum(_dot(p, ex) for p in _split2(dc))[0:1]
    acum_t = head_rows(acum)
    dt_t = head_rows(dt)

    xs_bf = xs.astype(BF16)
    xw_bf = (xs * ws_c).astype(BF16)
    h_prev = h_s[...]
    h_bf = h_prev.astype(BF16)

    ydiag, yoff, states = [], [], []
    for g in range(M_GROUPS):
        bg = bm[:, g * M_STATE:(g + 1) * M_STATE]
        cg = cm[:, g * M_STATE:(g + 1) * M_STATE]
        gcols = slice(g * GROUP_COLS, (g + 1) * GROUP_COLS)
        cb = _dot_nt(cg, bg)
        yoff.append(_dot(cg, h_bf[:, gcols]))
        states.append(_dot_tn(bg, xw_bf[:, gcols]))
        for pr in range(HEADS_PER_GROUP // 2):
            ms = []
            for r in range(2):
                hd = g * HEADS_PER_GROUP + pr * 2 + r
                acol = jnp.broadcast_to(acum[:, hd:hd + 1], (q, q))
                arow = jnp.broadcast_to(acum_t[hd:hd + 1, :], (q, q))
                dec = jnp.exp(jnp.where(causal, acol - arow, -1e30))
                ms.append((cb * dec * jnp.broadcast_to(dt_t[hd:hd + 1, :], (q, q))).astype(BF16))
            pair = g * (HEADS_PER_GROUP // 2) + pr
            xpair = xs_bf[:, pair * LANES:(pair + 1) * LANES]
            if q % LANES == 0:
                ydiag.append(_dot(jnp.concatenate(ms, axis=1), jnp.concatenate(_split_heads(xpair, first), axis=0)))
            else:
                ydiag.append(jnp.where(first, _dot(ms[0], xpair), _dot(ms[1], xpair)))
    y = jnp.concatenate(ydiag, axis=1) + jnp.concatenate(yoff, axis=1) * ey_c + dsk_ref[...] * xs
    y = y * zs_ref[0]
    normed = []
    for g in range(M_GROUPS):
        yg = y[:, g * GROUP_COLS:(g + 1) * GROUP_COLS]
        normed.append(yg * lax.rsqrt(jnp.mean(yg * yg, axis=-1, keepdims=True) + EPS))
    y_ref[0] = (jnp.concatenate(normed, axis=1) * mg_ref[...]).astype(BF16)
    h_s[...] = h_prev * dc_c + jnp.concatenate(states, axis=1)

    @pl.when(t == nt - 1)
    def _():
        for i in range(PAIRS):
            nssm_ref[0, i] = h_s[:, i * LANES:(i + 1) * LANES].T


def _ssd(xs, bc, zs, dt, h0, dtb, alog, dsk, mg, ex, *, q):
    bsz, L, _ = xs.shape
    grid = (bsz, L // q)
    h0p = h0.reshape(bsz, PAIRS, LANES, M_STATE)

    def tok(n):
        return pl.BlockSpec((1, q, n), lambda b, t: (b, t, 0))

    per_b = pl.BlockSpec((1, PAIRS, LANES, M_STATE), lambda b, t: (b, 0, 0, 0))
    y, nssm = pl.pallas_call(
        functools.partial(_ssd_kernel, q=q),
        grid=grid,
        in_specs=[tok(M_INNER), tok(BC_COLS), tok(M_INNER), tok(DT_PAD), per_b,
                  _const_spec((1, DT_PAD)), _const_spec((1, DT_PAD)), _const_spec((1, M_INNER)),
                  _const_spec((1, M_INNER)), _const_spec((DT_PAD, M_INNER))],
        out_specs=(tok(M_INNER), per_b),
        out_shape=(jax.ShapeDtypeStruct((bsz, L, M_INNER), BF16),
                   jax.ShapeDtypeStruct((bsz, PAIRS, LANES, M_STATE), F32)),
        scratch_shapes=[pltpu.VMEM((M_STATE, M_INNER), F32)],
        compiler_params=pltpu.CompilerParams(
            dimension_semantics=("parallel", "arbitrary"), vmem_limit_bytes=VMEM_LIMIT),
        name="ssd",
    )(xs, bc, zs, dt, h0p, dtb, alog, dsk, mg, ex)
    return y, nssm.reshape(bsz, M_HEADS, M_HEAD_DIM, M_STATE)


def _sb_weights(z, tri, carry, mask):
    u = z * LOG2_E
    neg_abs = lax.bitcast_convert_type(lax.bitcast_convert_type(u, jnp.uint32) | jnp.uint32(0x80000000), F32)
    drop = jnp.maximum(u, 0.0) + jnp.log2(1.0 + jnp.exp2(neg_abs))
    if mask is not None:
        drop = jnp.where(mask, drop, 0.0)
    cs = _dot(drop.astype(BF16), tri)
    w = jnp.exp2(u - cs + carry)
    if mask is not None:
        w = jnp.where(mask, w, 0.0)
    return w.astype(BF16), carry - cs[:, 0:1]


def _suffix_tri(n):
    r = lax.broadcasted_iota(jnp.int32, (n, n), 0)
    c = lax.broadcasted_iota(jnp.int32, (n, n), 1)
    return (r >= c).astype(BF16)


def _strictly_before(n):
    return lax.broadcasted_iota(jnp.int32, (n, n), 1) < lax.broadcasted_iota(jnp.int32, (n, n), 0)


def _live(c_s, nh):
    return jnp.max(functools.reduce(jnp.maximum, [c_s[i] for i in range(nh)]))


SB_STEP_HEADS = 16
PV_GROUP = 4


def _sb_prompt_kernel(q_ref, k_ref, v_ref, tri_ref, o_ref, acc_s, c_s, *, tb):
    qi = pl.program_id(2)
    nh = SB_STEP_HEADS
    width = nh * SB_HEAD_DIM
    first = lax.broadcasted_iota(jnp.int32, (tb, LANES), 1) < SB_HEAD_DIM
    lane_head = lax.broadcasted_iota(jnp.int32, (tb, PV_GROUP * SB_HEAD_DIM), 1) // SB_HEAD_DIM
    tri = tri_ref[...]
    diag_mask = _strictly_before(tb)
    qh = []
    for p in range(nh // 2):
        qh += list(_split_heads(q_ref[0, :, p * LANES:(p + 1) * LANES], first))

    def weights(j, mask, carries):
        rows = pl.ds(pl.multiple_of(j * tb, tb), tb)
        ws, cs = [], []
        for i in range(nh):
            k2 = k_ref[0, rows, (i // 2) * LANES:(i // 2 + 1) * LANES]
            w, c = _sb_weights(_dot_nt(qh[i], k2), tri, carries[i], mask)
            ws.append(w)
            cs.append(c)
        return ws, cs

    def pv(blocks):
        outs = []
        for g in range(nh // PV_GROUP):
            lanes = slice(g * PV_GROUP * SB_HEAD_DIM, (g + 1) * PV_GROUP * SB_HEAD_DIM)
            lhs, rhs = [], []
            for ws, j in blocks:
                v = v_ref[0, pl.ds(pl.multiple_of(j * tb, tb), tb), lanes]
                zero = jnp.zeros_like(v)
                for i in range(PV_GROUP):
                    lhs.append(ws[g * PV_GROUP + i])
                    rhs.append(jnp.where(lane_head == i, v, zero))
            outs.append(_dot(jnp.concatenate(lhs, axis=1), jnp.concatenate(rhs, axis=0)))
        return outs[0] if len(outs) == 1 else jnp.concatenate(outs, axis=1)

    def keep(cs):
        for i in range(nh):
            c_s[i] = cs[i]

    zero = [jnp.zeros((tb, 1), F32)] * nh

    @pl.when(qi == 0)
    def _():
        ws, cs = weights(0, diag_mask, zero)
        acc_s[...] = pv([(ws, 0)])
        keep(cs)

    @pl.when(qi > 0)
    def _():
        ws_d, cs = weights(qi, diag_mask, zero)
        ws_p, cs = weights(qi - 1, None, cs)
        acc_s[...] = pv([(ws_d, qi), (ws_p, qi - 1)])
        keep(cs)

    def cond(st):
        j, m = st
        return jnp.logical_and(j >= 0, m > SB_LOG_CUTOFF)

    def body(st):
        j, _ = st
        ws, cs = weights(j, None, [c_s[i] for i in range(nh)])
        acc_s[...] += pv([(ws, j)])
        keep(cs)
        return j - 1, _live(c_s, nh)

    lax.while_loop(cond, body, (qi - 2, _live(c_s, nh)))
    o_ref[0] = acc_s[...].astype(BF16)


def _sb_prompt(q, k, v, *, tb):
    bsz, L, _ = q.shape
    width = SB_STEP_HEADS * SB_HEAD_DIM
    grid = (bsz, SB_HEADS // SB_STEP_HEADS, L // tb)
    qspec = pl.BlockSpec((1, tb, width), lambda b, h, i: (b, i, h))
    kspec = pl.BlockSpec((1, L, width), lambda b, h, i: (b, 0, h))
    return pl.pallas_call(
        functools.partial(_sb_prompt_kernel, tb=tb),
        grid=grid,
        in_specs=[qspec, kspec, kspec, _const_spec((tb, tb))],
        out_specs=qspec,
        out_shape=jax.ShapeDtypeStruct((bsz, L, SB_WIDTH), BF16),
        scratch_shapes=[pltpu.VMEM((tb, width), F32), pltpu.VMEM((SB_STEP_HEADS, tb, 1), F32)],
        compiler_params=pltpu.CompilerParams(
            dimension_semantics=("parallel", "parallel", "arbitrary"), vmem_limit_bytes=VMEM_LIMIT),
        name="sb_prompt",
    )(q, k, v, _suffix_tri(tb))


def _sb_sample_kernel(q_ref, k_ref, v_ref, pk_ref, pv_ref, tri_d_ref, tri_p_ref, o_ref, acc_s, c_s, *, tq, tk,
                      npast):
    nh = PV_GROUP
    width = nh * SB_HEAD_DIM
    q2 = q_ref[0]
    first = lax.broadcasted_iota(jnp.int32, (tq, LANES), 1) < SB_HEAD_DIM
    lane_head = lax.broadcasted_iota(jnp.int32, (tq, width), 1) // SB_HEAD_DIM
    row_head = lax.broadcasted_iota(jnp.int32, (width, tk), 0) // SB_HEAD_DIM
    tri_p = tri_p_ref[...]
    qh = []
    for p in range(nh // 2):
        qh += list(_split_heads(q2[:, p * LANES:(p + 1) * LANES], first))

    def cache_block(j, carries):
        cols = pl.ds(pl.multiple_of(j * tk, tk), tk)
        ws, cs = [], []
        for i in range(nh):
            qs = q2[:, i * SB_HEAD_DIM:(i + 1) * SB_HEAD_DIM]
            w, c = _sb_weights(_dot(qs, pk_ref[0, i, :, cols].astype(BF16)), tri_p, carries[i], None)
            ws.append(w)
            cs.append(c)
        vt = pv_ref[0, :, :, cols].astype(BF16).reshape(width, tk)
        zero = jnp.zeros_like(vt)
        vts = jnp.concatenate([jnp.where(row_head == i, vt, zero) for i in range(nh)], axis=1)
        return jnp.concatenate(ws, axis=1), cs, vts

    ws_d, cs = [], []
    for i in range(nh):
        k2 = k_ref[0, :, (i // 2) * LANES:(i // 2 + 1) * LANES]
        w, c = _sb_weights(_dot_nt(qh[i], k2), tri_d_ref[...], jnp.zeros((tq, 1), F32), _strictly_before(tq))
        ws_d.append(w)
        cs.append(c)
    vnew = v_ref[0]
    vals_d = jnp.concatenate([jnp.where(lane_head == i, vnew, jnp.zeros_like(vnew)) for i in range(nh)], axis=0)
    ws_p, cs, vts = cache_block(npast - 1, cs)
    acc_s[...] = _dot(jnp.concatenate(ws_d, axis=1), vals_d) + _dot_nt(ws_p, vts)
    for i in range(nh):
        c_s[i] = cs[i]

    def cond(st):
        j, m = st
        return jnp.logical_and(j >= 0, m > SB_LOG_CUTOFF)

    def body(st):
        j, _ = st
        ws, cs, vts = cache_block(j, [c_s[i] for i in range(nh)])
        acc_s[...] += _dot_nt(ws, vts)
        for i in range(nh):
            c_s[i] = cs[i]
        return j - 1, _live(c_s, nh)

    lax.while_loop(cond, body, (jnp.int32(npast - 2), _live(c_s, nh)))
    o_ref[0] = acc_s[...].astype(BF16)


def _sb_sample(q, k, v, past_kt, past_vt, *, tk):
    bsz, tq, _ = q.shape
    past = past_kt.shape[3]
    width = PV_GROUP * SB_HEAD_DIM
    grid = (bsz, SB_HEADS // PV_GROUP)
    qspec = pl.BlockSpec((1, tq, width), lambda b, h: (b, 0, h))
    pspec = pl.BlockSpec((1, PV_GROUP, SB_HEAD_DIM, past), lambda b, h: (b, h, 0, 0))
    return pl.pallas_call(
        functools.partial(_sb_sample_kernel, tq=tq, tk=tk, npast=past // tk),
        grid=grid,
        in_specs=[qspec, qspec, qspec, pspec, pspec, _const_spec((tq, tq)), _const_spec((tk, tk))],
        out_specs=qspec,
        out_shape=jax.ShapeDtypeStruct((bsz, tq, SB_WIDTH), BF16),
        scratch_shapes=[pltpu.VMEM((tq, width), F32), pltpu.VMEM((PV_GROUP, tq, 1), F32)],
        compiler_params=pltpu.CompilerParams(
            dimension_semantics=("parallel", "parallel"), vmem_limit_bytes=VMEM_LIMIT),
        name="sb_sample",
    )(q, k, v, past_kt, past_vt, _suffix_tri(tq), _suffix_tri(tk))


def _post_kernel(x_ref, y_ref, o_ref, ga_ref, gb_ref, p_ref, wa_ref, wb_ref, wo_ref, gf_ref, wg_ref, wu_ref,
                 wd_ref, gp_ref, wpg_ref, wp_ref, gl_ref, out_ref, *, final_norm):
    branch_a = _dot(y_ref[...], wa_ref[...])
    branch_b = _dot(o_ref[...], wb_ref[...])
    merged = jax.nn.sigmoid(ga_ref[...]) * branch_a + jax.nn.sigmoid(gb_ref[...]) * branch_b
    x = x_ref[...] + _dot(merged.astype(BF16), wo_ref[...])
    hf = _rms(x, gf_ref[...]).astype(BF16)
    gate = _dot(hf, wg_ref[...])
    act = (gate * jax.nn.sigmoid(gate) * _dot(hf, wu_ref[...])).astype(BF16)
    x = x + _dot(act, wd_ref[...])
    g = jax.nn.sigmoid(_dot(_rms(x, gp_ref[...]).astype(BF16), wpg_ref[...]))
    x = x + g * _dot(p_ref[...].astype(BF16), wp_ref[...])
    if final_norm:
        x = _rms(x, gl_ref[...])
    out_ref[...] = x


def _post(x, y, o, ga, gb, p, wts, *, tm, final_norm):
    T = x.shape[0]

    def tok(n):
        return pl.BlockSpec((tm, n), lambda i: (i, 0))

    wspecs = [_const_spec(w.shape) for w in wts]
    return pl.pallas_call(
        functools.partial(_post_kernel, final_norm=final_norm),
        grid=(T // tm,),
        in_specs=[tok(D_MODEL), tok(M_INNER), tok(SB_WIDTH), tok(D_MODEL), tok(D_MODEL), tok(PLE_DIM)] + wspecs,
        out_specs=tok(D_MODEL),
        out_shape=jax.ShapeDtypeStruct((T, D_MODEL), F32),
        compiler_params=pltpu.CompilerParams(
            dimension_semantics=("parallel",), vmem_limit_bytes=VMEM_LIMIT),
        name="post",
    )(x, y, o, ga, gb, p, *wts)


def _pack_w_in(w_in):
    dt_cols = jnp.pad(w_in[:, OFF_DT:OFF_Q], ((0, 0), (0, DT_PAD - M_HEADS)))
    parts = [w_in[:, OFF_Z:OFF_DT], dt_cols, w_in[:, OFF_Q:OFF_K] * SB_SCALE, w_in[:, OFF_K:IN_COLS]]
    return jnp.concatenate(parts, axis=1).astype(BF16)


def _head_expand():
    r = lax.broadcasted_iota(jnp.int32, (DT_PAD, M_INNER), 0)
    c = lax.broadcasted_iota(jnp.int32, (DT_PAD, M_INNER), 1)
    return (c // M_HEAD_DIM == r).astype(BF16)


def _pad_heads(v):
    return jnp.pad(v.reshape(1, M_HEADS).astype(F32), ((0, 0), (0, DT_PAD - M_HEADS)))


def _layer(x, p, past_k, past_v, conv_hist, h0, prm, *, final_norm):
    bsz, L, _ = x.shape
    prompt = past_k is None
    if prompt:
        nb, tl = 1, min(L, TOKEN_TILE)
        conv_hist = jnp.zeros((bsz, CONV_W - 1, CONV_DIM), F32)
        h0 = jnp.zeros((bsz, M_HEADS, M_HEAD_DIM, M_STATE), F32)
    else:
        nb, tl = bsz, L
    zs, xs, bc, dt, q, k, v, new_k, new_v, ga, gb, new_conv = _inproj(
        x, conv_hist.astype(F32), prm["norm_mix_g"], prm["w_in"], prm["conv_w"], prm["conv_b"],
        nb=nb, tl=tl, kv_feature_major=prompt)
    if prompt:
        new_k, new_v = jnp.swapaxes(new_k, 2, 3), jnp.swapaxes(new_v, 2, 3)

    y, new_ssm = _ssd(xs, bc, zs, dt, h0.astype(F32), prm["dt_bias"], prm["a_log"], prm["d_skip"], prm["mnorm_g"],
                      prm["head_expand"], q=min(L, SSD_CHUNK))
    if prompt:
        o = _sb_prompt(q, k, v, tb=min(L, SB_BLOCK))
    else:
        o = _sb_sample(q, k, v, jnp.swapaxes(past_k.astype(F32), 2, 3), jnp.swapaxes(past_v.astype(F32), 2, 3),
                       tk=min(past_k.shape[2], SB_BLOCK))

    T = bsz * L
    flat = lambda a: a.reshape(T, a.shape[-1])
    x = _post(flat(x), flat(y), flat(o), flat(ga), flat(gb), flat(p), prm["post"], tm=min(T, TOKEN_TILE),
              final_norm=final_norm)
    return x.reshape(bsz, L, D_MODEL), new_k, new_v, new_conv, new_ssm


def _layer_params(i, norm_mix_g, w_in, conv_w, conv_b, dt_bias, a_log, d_skip, mnorm_g, w_a, w_b, w_out,
                  norm_ffn_g, w_gate, w_up, w_down, norm_ple_g, w_ple_gate, w_ple, final_norm_g):
    row = lambda v: v.reshape(1, -1).astype(F32)
    bf = lambda w: w.astype(BF16)
    return {
        "norm_mix_g": row(norm_mix_g[i]),
        "w_in": _pack_w_in(w_in[i]),
        "conv_w": conv_w[i].astype(F32),
        "conv_b": row(conv_b[i]),
        "dt_bias": _pad_heads(dt_bias[i]),
        "a_log": _pad_heads(a_log[i]),
        "d_skip": row(jnp.repeat(d_skip[i], M_HEAD_DIM)),
        "mnorm_g": row(mnorm_g[i]),
        "head_expand": _head_expand(),
        "post": (bf(w_a[i]), bf(w_b[i]), bf(w_out[i]), row(norm_ffn_g[i]), bf(w_gate[i]), bf(w_up[i]),
                 bf(w_down[i]), row(norm_ple_g[i]), bf(w_ple_gate[i]), bf(w_ple[i]), row(final_norm_g)),
    }


def kernel(x_prompt, x_sample, cache_k, cache_v, state_conv, state_ssm, p_prompt, p_sample, norm_mix_g, w_in,
           conv_w, conv_b, dt_bias, a_log, d_skip, mnorm_g, w_a, w_b, w_out, norm_ffn_g, w_gate, w_up, w_down,
           norm_ple_g, w_ple_gate, w_ple, final_norm_g):
    depth = w_in.shape[0]
    xp, xs = x_prompt, x_sample
    outs_p, outs_s = [], []
    for i in range(depth):
        prm = _layer_params(i, norm_mix_g, w_in, conv_w, conv_b, dt_bias, a_log, d_skip, mnorm_g, w_a, w_b,
                            w_out, norm_ffn_g, w_gate, w_up, w_down, norm_ple_g, w_ple_gate, w_ple, final_norm_g)
        last = i == depth - 1
        xp, *st_p = _layer(xp, p_prompt[i], None, None, None, None, prm, final_norm=last)
        xs, *st_s = _layer(xs, p_sample[i], cache_k[i], cache_v[i], state_conv[i], state_ssm[i], prm,
                           final_norm=last)
        outs_p.append(st_p)
        outs_s.append(st_s)
    stack = lambda outs, j: jnp.stack([o[j] for o in outs])
    return (xp, xs,
            stack(outs_p, 0), stack(outs_p, 1), stack(outs_p, 2), stack(outs_p, 3),
            stack(outs_s, 0), stack(outs_s, 1), stack(outs_s, 2), stack(outs_s, 3))
```

```python
import functools

import jax
import jax.numpy as jnp
from jax import lax
from jax.experimental import pallas as pl
from jax.experimental.pallas import tpu as pltpu

F32 = jnp.float32
BF16 = jnp.bfloat16

EPS = 1e-6
D_MODEL = 1024
M_HEAD_DIM = 64
M_HEADS = 16
M_GROUPS = 4
M_STATE = 128
M_INNER = M_HEADS * M_HEAD_DIM
CONV_W = 4
CONV_DIM = M_INNER + 2 * M_GROUPS * M_STATE
SB_HEADS = 16
SB_HEAD_DIM = 64
SB_WIDTH = SB_HEADS * SB_HEAD_DIM
SB_SCALE = SB_HEAD_DIM ** -0.5
D_FF = 2816
PLE_DIM = 256
OFF_Z = 0
OFF_XBC = OFF_Z + M_INNER
OFF_DT = OFF_XBC + CONV_DIM
OFF_Q = OFF_DT + M_HEADS
OFF_K = OFF_Q + SB_WIDTH
OFF_V = OFF_K + SB_WIDTH
OFF_GA = OFF_V + SB_WIDTH
OFF_GB = OFF_GA + D_MODEL
IN_COLS = OFF_GB + D_MODEL

LANES = 128
DT_PAD = LANES
P_Z = 0
P_XBC = P_Z + M_INNER
P_DT = P_XBC + CONV_DIM
P_Q = P_DT + DT_PAD
P_K = P_Q + SB_WIDTH
P_V = P_K + SB_WIDTH
P_GA = P_V + SB_WIDTH
P_GB = P_GA + D_MODEL
P_COLS = P_GB + D_MODEL

VMEM_LIMIT = 56 * 1024 * 1024
LOG2_E = 1.4426950408889634
SB_LOG_CUTOFF = -150.0
TOKEN_TILE = 256
SB_BLOCK = 256
SSD_CHUNK = 256


def _const_spec(shape):
    nd = len(shape)
    return pl.BlockSpec(shape, lambda *_: (0,) * nd, pipeline_mode=pl.Buffered(1))


def _rms(x, g):
    ms = jnp.mean(x * x, axis=-1, keepdims=True)
    return x * lax.rsqrt(ms + EPS) * g


def _split2(x):
    hi = x.astype(BF16)
    lo = (x - hi.astype(F32)).astype(BF16)
    return hi, lo


def _split3(x):
    hi = x.astype(BF16)
    r = x - hi.astype(F32)
    mid = r.astype(BF16)
    lo = (r - mid.astype(F32)).astype(BF16)
    return hi, mid, lo


def _dot(a, b):
    return jnp.dot(a, b, preferred_element_type=F32)


def _dot_nt(a, b):
    return lax.dot_general(a, b, (((1,), (1,)), ((), ())), preferred_element_type=F32)


def _dot_tn(a, b):
    return lax.dot_general(a, b, (((0,), (0,)), ((), ())), preferred_element_type=F32)


def _split_heads(x2, first):
    zero = jnp.zeros_like(x2)
    return jnp.where(first, x2, zero), jnp.where(first, zero, x2)


HIST_ROW = 8 - (CONV_W - 1)
BC_COLS = 2 * M_GROUPS * M_STATE
CONV_STRIP = 512


def _inproj_kernel(x_ref, hist_ref, g_ref, w_ref, cw_ref, cb_ref, zs_ref, xs_ref, bc_ref, dt_ref, q_ref, k_ref,
                   v_ref, ko_ref, vo_ref, ga_ref, gb_ref, nconv_ref, xp_s, *, nb, tl, kv_feature_major):
    t = pl.program_id(1)
    tm = nb * tl
    x = x_ref[...].reshape(tm, D_MODEL)
    h = _rms(x, g_ref[...]).astype(BF16)

    def proj(lo, n):
        return _dot(h, w_ref[:, lo:lo + n])

    @pl.when(t == 0)
    def _():
        xp_s[:, HIST_ROW:8, :] = hist_ref[...]

    def conv_cols(c0):
        cols = slice(c0, c0 + CONV_STRIP)
        xp_s[:, 8:8 + tl, cols] = proj(P_XBC + c0, CONV_STRIP).reshape(nb, tl, CONV_STRIP)
        xp = xp_s[:, :, cols]
        xc = cb_ref[:, cols] + xp[:, 8:8 + tl, :] * cw_ref[CONV_W - 1:CONV_W, cols]
        for k in range(1, CONV_W):
            xc = xc + pltpu.roll(xp, k, axis=1)[:, 8:8 + tl, :] * cw_ref[CONV_W - 1 - k:CONV_W - k, cols]
        xc = xc * jax.nn.sigmoid(xc)
        if c0 < M_INNER:
            xs_ref[:, :, cols] = xc
        else:
            bc_ref[:, :, c0 - M_INNER:c0 - M_INNER + CONV_STRIP] = xc.astype(BF16)
        tail = xp_s[:, 8 + tl - (CONV_W - 1):8 + tl, cols]
        nconv_ref[:, :, cols] = tail
        xp_s[:, HIST_ROW:8, cols] = tail

    def kv_proj(src, bf_ref, out_ref):
        kv = proj(src, SB_WIDTH)
        bf_ref[...] = kv.astype(BF16).reshape(nb, tl, SB_WIDTH)
        if kv_feature_major:
            out_ref[0] = kv.T.reshape(SB_HEADS, SB_HEAD_DIM, tl)
        else:
            for b in range(nb):
                for hd in range(SB_HEADS):
                    out_ref[b, hd] = kv[b * tl:(b + 1) * tl, hd * SB_HEAD_DIM:(hd + 1) * SB_HEAD_DIM]

    def z_proj():
        z = proj(P_Z, M_INNER)
        zs_ref[...] = (z * jax.nn.sigmoid(z)).reshape(nb, tl, M_INNER)

    def q_proj():
        dt_ref[...] = proj(P_DT, DT_PAD).reshape(nb, tl, DT_PAD)
        q_ref[...] = proj(P_Q, SB_WIDTH).astype(BF16).reshape(nb, tl, SB_WIDTH)

    def gate_proj(src, out_ref):
        out_ref[...] = proj(src, D_MODEL).reshape(nb, tl, D_MODEL)

    others = [z_proj, q_proj, functools.partial(kv_proj, P_K, k_ref, ko_ref),
              functools.partial(kv_proj, P_V, v_ref, vo_ref), functools.partial(gate_proj, P_GA, ga_ref),
              functools.partial(gate_proj, P_GB, gb_ref)]
    strips = list(range(0, CONV_DIM, CONV_STRIP))
    for i in range(max(len(others), len(strips))):
        if i < len(strips):
            conv_cols(strips[i])
        if i < len(others):
            others[i]()


def _inproj(x, hist, g, w, cw, cb, *, nb, tl, kv_feature_major):
    bsz, L, _ = x.shape
    grid = (bsz // nb, L // tl)

    def tok(n):
        return pl.BlockSpec((nb, tl, n), lambda b, t: (b, t, 0))

    if kv_feature_major:
        assert nb == 1
        head_shape = (bsz, SB_HEADS, SB_HEAD_DIM, L)
        head = pl.BlockSpec((1, SB_HEADS, SB_HEAD_DIM, tl), lambda b, t: (b, 0, 0, t))
    else:
        head_shape = (bsz, SB_HEADS, L, SB_HEAD_DIM)
        head = pl.BlockSpec((nb, SB_HEADS, tl, SB_HEAD_DIM), lambda b, t: (b, 0, t, 0))
    hist_spec = pl.BlockSpec((nb, CONV_W - 1, CONV_DIM), lambda b, t: (b, 0, 0))
    out_shape = (
        jax.ShapeDtypeStruct((bsz, L, M_INNER), F32),
        jax.ShapeDtypeStruct((bsz, L, M_INNER), F32),
        jax.ShapeDtypeStruct((bsz, L, BC_COLS), BF16),
        jax.ShapeDtypeStruct((bsz, L, DT_PAD), F32),
        jax.ShapeDtypeStruct((bsz, L, SB_WIDTH), BF16),
        jax.ShapeDtypeStruct((bsz, L, SB_WIDTH), BF16),
        jax.ShapeDtypeStruct((bsz, L, SB_WIDTH), BF16),
        jax.ShapeDtypeStruct(head_shape, F32),
        jax.ShapeDtypeStruct(head_shape, F32),
        jax.ShapeDtypeStruct((bsz, L, D_MODEL), F32),
        jax.ShapeDtypeStruct((bsz, L, D_MODEL), F32),
        jax.ShapeDtypeStruct((bsz, CONV_W - 1, CONV_DIM), F32),
    )
    out_specs = (tok(M_INNER), tok(M_INNER), tok(BC_COLS), tok(DT_PAD), tok(SB_WIDTH), tok(SB_WIDTH), tok(SB_WIDTH),
                 head, head, tok(D_MODEL), tok(D_MODEL), hist_spec)
    return pl.pallas_call(
        functools.partial(_inproj_kernel, nb=nb, tl=tl, kv_feature_major=kv_feature_major),
        grid=grid,
        in_specs=[tok(D_MODEL), hist_spec, _const_spec((1, D_MODEL)), _const_spec((D_MODEL, P_COLS)),
                  _const_spec((CONV_W, CONV_DIM)), _const_spec((1, CONV_DIM))],
        out_specs=out_specs,
        out_shape=out_shape,
        scratch_shapes=[pltpu.VMEM((nb, 8 + tl, CONV_DIM), F32)],
        compiler_params=pltpu.CompilerParams(
            dimension_semantics=("parallel", "arbitrary"), vmem_limit_bytes=VMEM_LIMIT),
        name="inproj",
    )(x, hist, g, w, cw, cb)


PAIRS = M_HEADS // 2
HEADS_PER_GROUP = M_HEADS // M_GROUPS
GROUP_COLS = HEADS_PER_GROUP * M_HEAD_DIM
SSD_BATCHES = 4


def _ssd_kernel(xs_ref, bc_ref, zs_ref, dt_ref, h0_ref, dtb_ref, alog_ref, dsk_ref, mg_ref, ex_ref,
                y_ref, nssm_ref, h_s, *, q, nbat):
    t = pl.program_id(1)
    nt = pl.num_programs(1)

    @pl.when(t == 0)
    def _():
        for b in range(nbat):
            for i in range(PAIRS):
                h_s[b, :, i * LANES:(i + 1) * LANES] = h0_ref[b, i].T

    ri = lax.broadcasted_iota(jnp.int32, (q, q), 0)
    ci = lax.broadcasted_iota(jnp.int32, (q, q), 1)
    causal = ri >= ci
    tri = causal.astype(BF16)
    first = lax.broadcasted_iota(jnp.int32, (q, LANES), 1) < M_HEAD_DIM
    ex = ex_ref[...]

    def head_rows(a):
        if q % LANES:
            a = jnp.concatenate([a, jnp.zeros((LANES - q, LANES), F32)], axis=0)
        return a.T[:, :q]

    for b in range(nbat):
        _ssd_chunk(b, xs_ref, bc_ref, zs_ref, dt_ref, dtb_ref, alog_ref, dsk_ref, mg_ref, y_ref, h_s,
                   q, causal, tri, first, ex, head_rows)

    @pl.when(t == nt - 1)
    def _():
        for b in range(nbat):
            for i in range(PAIRS):
                nssm_ref[b, i] = h_s[b, :, i * LANES:(i + 1) * LANES].T


def _ssd_chunk(b, xs_ref, bc_ref, zs_ref, dt_ref, dtb_ref, alog_ref, dsk_ref, mg_ref, y_ref, h_s,
               q, causal, tri, first, ex, head_rows):
    xs = xs_ref[b]
    bm = bc_ref[b, :, 0:M_GROUPS * M_STATE]
    cm = bc_ref[b, :, M_GROUPS * M_STATE:BC_COLS]
    xdt = dt_ref[b] + dtb_ref[...]
    dt = jnp.maximum(xdt, 0.0) + jnp.log1p(jnp.exp(-jnp.abs(xdt)))
    da = dt * (-jnp.exp(alog_ref[...]))

    acum = sum(_dot(tri, p) for p in _split3(da))
    tot = acum[q - 1:q, :]
    ey = jnp.exp(acum)
    ws = jnp.exp(tot - acum) * dt
    dc = jnp.broadcast_to(jnp.exp(tot), (8, DT_PAD))
    stack_c = _dot(jnp.concatenate([ey, ws], axis=0).astype(BF16), ex)
    ey_c = stack_c[0:q]
    ws_c = stack_c[q:2 * q]
    dc_c = sum(_dot(p, ex) for p in _split2(dc))[0:1]
    acum2 = acum * LOG2_E
    acum2_t = head_rows(acum2)
    dt_t = head_rows(dt)

    def decay_weights(cb, hd, r0, nr, nc):
        acol = jnp.broadcast_to(acum2[r0:r0 + nr, hd:hd + 1], (nr, nc))
        arow = jnp.broadcast_to(acum2_t[hd:hd + 1, 0:nc], (nr, nc))
        dec = jnp.exp2(jnp.where(causal[r0:r0 + nr, 0:nc], acol - arow, -1e30))
        return (cb[r0:r0 + nr, 0:nc] * dec * jnp.broadcast_to(dt_t[hd:hd + 1, 0:nc], (nr, nc))).astype(BF16)

    bands = ((0, q // 2, q // 2), (q // 2, q // 2, q)) if q % (2 * LANES) == 0 else ((0, q, q),)

    xs_bf = xs.astype(BF16)
    xw_bf = (xs * ws_c).astype(BF16)
    h_prev = h_s[b]
    h_bf = h_prev.astype(BF16)

    ydiag, yoff, states = [], [], []
    for g in range(M_GROUPS):
        bg = bm[:, g * M_STATE:(g + 1) * M_STATE]
        cg = cm[:, g * M_STATE:(g + 1) * M_STATE]
        gcols = slice(g * GROUP_COLS, (g + 1) * GROUP_COLS)
        cb = _dot_nt(cg, bg)
        yoff.append(_dot(cg, h_bf[:, gcols]))
        states.append(_dot_tn(bg, xw_bf[:, gcols]))
        for pr in range(HEADS_PER_GROUP // 2):
            pair = g * (HEADS_PER_GROUP // 2) + pr
            xpair = xs_bf[:, pair * LANES:(pair + 1) * LANES]
            xe, xo = _split_heads(xpair, first)
            rows = []
            for r0, nr, nc in bands:
                ms = [decay_weights(cb, 2 * pair + r, r0, nr, nc) for r in range(2)]
                if q % LANES == 0:
                    rows.append(_dot(jnp.concatenate(ms, axis=1), jnp.concatenate([xe[0:nc], xo[0:nc]], axis=0)))
                else:
                    rows.append(jnp.where(first, _dot(ms[0], xpair), _dot(ms[1], xpair)))
            ydiag.append(rows[0] if len(rows) == 1 else jnp.concatenate(rows, axis=0))
    y = jnp.concatenate(ydiag, axis=1) + jnp.concatenate(yoff, axis=1) * ey_c + dsk_ref[...] * xs
    y = y * zs_ref[b]
    normed = []
    for g in range(M_GROUPS):
        yg = y[:, g * GROUP_COLS:(g + 1) * GROUP_COLS]
        normed.append(yg * lax.rsqrt(jnp.mean(yg * yg, axis=-1, keepdims=True) + EPS))
    y_ref[b] = (jnp.concatenate(normed, axis=1) * mg_ref[...]).astype(BF16)
    h_s[b] = h_prev * dc_c + jnp.concatenate(states, axis=1)


def _ssd(xs, bc, zs, dt, h0, dtb, alog, dsk, mg, ex, *, q):
    bsz, L, _ = xs.shape
    nbat = SSD_BATCHES if bsz % SSD_BATCHES == 0 else 1
    grid = (bsz // nbat, L // q)
    h0p = h0.reshape(bsz, PAIRS, LANES, M_STATE)

    def tok(n):
        return pl.BlockSpec((nbat, q, n), lambda b, t: (b, t, 0))

    per_b = pl.BlockSpec((nbat, PAIRS, LANES, M_STATE), lambda b, t: (b, 0, 0, 0))
    y, nssm = pl.pallas_call(
        functools.partial(_ssd_kernel, q=q, nbat=nbat),
        grid=grid,
        in_specs=[tok(M_INNER), tok(BC_COLS), tok(M_INNER), tok(DT_PAD), per_b,
                  _const_spec((1, DT_PAD)), _const_spec((1, DT_PAD)), _const_spec((1, M_INNER)),
                  _const_spec((1, M_INNER)), _const_spec((DT_PAD, M_INNER))],
        out_specs=(tok(M_INNER), per_b),
        out_shape=(jax.ShapeDtypeStruct((bsz, L, M_INNER), BF16),
                   jax.ShapeDtypeStruct((bsz, PAIRS, LANES, M_STATE), F32)),
        scratch_shapes=[pltpu.VMEM((nbat, M_STATE, M_INNER), F32)],
        compiler_params=pltpu.CompilerParams(
            dimension_semantics=("parallel", "arbitrary"), vmem_limit_bytes=VMEM_LIMIT),
        name="ssd",
    )(xs, bc, zs, dt, h0p, dtb, alog, dsk, mg, ex)
    return y, nssm.reshape(bsz, M_HEADS, M_HEAD_DIM, M_STATE)


def _sb_weights(u, tri, carry, mask):
    neg_abs = lax.bitcast_convert_type(lax.bitcast_convert_type(u, jnp.uint32) | jnp.uint32(0x80000000), F32)
    drop = jnp.maximum(u, 0.0) + jnp.log2(1.0 + jnp.exp2(neg_abs))
    if mask is not None:
        drop = jnp.where(mask, drop, 0.0)
    cs = _dot(drop.astype(BF16), tri)
    w = jnp.exp2(u - cs + carry)
    if mask is not None:
        w = jnp.where(mask, w, 0.0)
    return w.astype(BF16), carry - cs[:, 0:1]


def _suffix_tri(n):
    r = lax.broadcasted_iota(jnp.int32, (n, n), 0)
    c = lax.broadcasted_iota(jnp.int32, (n, n), 1)
    return (r >= c).astype(BF16)


def _strictly_before(n):
    return lax.broadcasted_iota(jnp.int32, (n, n), 1) < lax.broadcasted_iota(jnp.int32, (n, n), 0)


def _live(c_s, nh):
    return jnp.max(functools.reduce(jnp.maximum, [c_s[i] for i in range(nh)]))


SB_STEP_HEADS = 16
PV_GROUP = 4


def _sb_prompt_kernel(q_ref, k_ref, v_ref, tri_ref, o_ref, acc_s, c_s, *, tb):
    qi = pl.program_id(2)
    nh = SB_STEP_HEADS
    width = nh * SB_HEAD_DIM
    first = lax.broadcasted_iota(jnp.int32, (tb, LANES), 1) < SB_HEAD_DIM
    lane_head = lax.broadcasted_iota(jnp.int32, (tb, PV_GROUP * SB_HEAD_DIM), 1) // SB_HEAD_DIM
    tri = tri_ref[...]
    diag_mask = _strictly_before(tb)
    qh = []
    for p in range(nh // 2):
        qh += list(_split_heads(q_ref[0, :, p * LANES:(p + 1) * LANES], first))

    def weights(j, mask, carries):
        rows = pl.ds(pl.multiple_of(j * tb, tb), tb)
        ws, cs = [], []
        for i in range(nh):
            k2 = k_ref[0, rows, (i // 2) * LANES:(i // 2 + 1) * LANES]
            w, c = _sb_weights(_dot_nt(qh[i], k2), tri, carries[i], mask)
            ws.append(w)
            cs.append(c)
        return ws, cs

    def pv(blocks):
        outs = []
        for g in range(nh // PV_GROUP):
            lanes = slice(g * PV_GROUP * SB_HEAD_DIM, (g + 1) * PV_GROUP * SB_HEAD_DIM)
            lhs, rhs = [], []
            for ws, j in blocks:
                v = v_ref[0, pl.ds(pl.multiple_of(j * tb, tb), tb), lanes]
                zero = jnp.zeros_like(v)
                for i in range(PV_GROUP):
                    lhs.append(ws[g * PV_GROUP + i])
                    rhs.append(jnp.where(lane_head == i, v, zero))
            outs.append(_dot(jnp.concatenate(lhs, axis=1), jnp.concatenate(rhs, axis=0)))
        return outs[0] if len(outs) == 1 else jnp.concatenate(outs, axis=1)

    def keep(cs):
        for i in range(nh):
            c_s[i] = cs[i]

    zero = [jnp.zeros((tb, 1), F32)] * nh

    @pl.when(qi == 0)
    def _():
        ws, cs = weights(0, diag_mask, zero)
        acc_s[...] = pv([(ws, 0)])
        keep(cs)

    @pl.when(qi > 0)
    def _():
        ws_d, cs = weights(qi, diag_mask, zero)
        ws_p, cs = weights(qi - 1, None, cs)
        acc_s[...] = pv([(ws_d, qi), (ws_p, qi - 1)])
        keep(cs)

    def cond(st):
        j, m = st
        return jnp.logical_and(j >= 0, m > SB_LOG_CUTOFF)

    def body(st):
        j, _ = st
        ws, cs = weights(j, None, [c_s[i] for i in range(nh)])
        acc_s[...] += pv([(ws, j)])
        keep(cs)
        return j - 1, _live(c_s, nh)

    lax.while_loop(cond, body, (qi - 2, _live(c_s, nh)))
    o_ref[0] = acc_s[...].astype(BF16)


def _sb_prompt(q, k, v, *, tb):
    bsz, L, _ = q.shape
    width = SB_STEP_HEADS * SB_HEAD_DIM
    grid = (bsz, SB_HEADS // SB_STEP_HEADS, L // tb)
    qspec = pl.BlockSpec((1, tb, width), lambda b, h, i: (b, i, h))
    kspec = pl.BlockSpec((1, L, width), lambda b, h, i: (b, 0, h))
    return pl.pallas_call(
        functools.partial(_sb_prompt_kernel, tb=tb),
        grid=grid,
        in_specs=[qspec, kspec, kspec, _const_spec((tb, tb))],
        out_specs=qspec,
        out_shape=jax.ShapeDtypeStruct((bsz, L, SB_WIDTH), BF16),
        scratch_shapes=[pltpu.VMEM((tb, width), F32), pltpu.VMEM((SB_STEP_HEADS, tb, 1), F32)],
        compiler_params=pltpu.CompilerParams(
            dimension_semantics=("parallel", "parallel", "arbitrary"), vmem_limit_bytes=VMEM_LIMIT),
        name="sb_prompt",
    )(q, k, v, _suffix_tri(tb))


def _sb_sample_kernel(q_ref, k_ref, v_ref, pk_ref, pv_ref, tri_d_ref, tri_p_ref, o_ref, acc_s, c_s, *, tq, tk,
                      npast):
    nh = PV_GROUP
    width = nh * SB_HEAD_DIM
    q2 = q_ref[0]
    first = lax.broadcasted_iota(jnp.int32, (tq, LANES), 1) < SB_HEAD_DIM
    lane_head = lax.broadcasted_iota(jnp.int32, (tq, width), 1) // SB_HEAD_DIM
    row_head = lax.broadcasted_iota(jnp.int32, (width, tk), 0) // SB_HEAD_DIM
    tri_p = tri_p_ref[...]
    qh = []
    for p in range(nh // 2):
        qh += list(_split_heads(q2[:, p * LANES:(p + 1) * LANES], first))

    def cache_block(j, carries):
        cols = pl.ds(pl.multiple_of(j * tk, tk), tk)
        ws, cs = [], []
        for i in range(nh):
            qs = q2[:, i * SB_HEAD_DIM:(i + 1) * SB_HEAD_DIM]
            w, c = _sb_weights(_dot(qs, pk_ref[0, i, :, cols].astype(BF16)), tri_p, carries[i], None)
            ws.append(w)
            cs.append(c)
        vt = pv_ref[0, :, :, cols].astype(BF16).reshape(width, tk)
        zero = jnp.zeros_like(vt)
        vts = jnp.concatenate([jnp.where(row_head == i, vt, zero) for i in range(nh)], axis=1)
        return jnp.concatenate(ws, axis=1), cs, vts

    ws_d, cs = [], []
    for i in range(nh):
        k2 = k_ref[0, :, (i // 2) * LANES:(i // 2 + 1) * LANES]
        w, c = _sb_weights(_dot_nt(qh[i], k2), tri_d_ref[...], jnp.zeros((tq, 1), F32), _strictly_before(tq))
        ws_d.append(w)
        cs.append(c)
    vnew = v_ref[0]
    vals_d = jnp.concatenate([jnp.where(lane_head == i, vnew, jnp.zeros_like(vnew)) for i in range(nh)], axis=0)
    ws_p, cs, vts = cache_block(npast - 1, cs)
    acc_s[...] = _dot(jnp.concatenate(ws_d, axis=1), vals_d) + _dot_nt(ws_p, vts)
    for i in range(nh):
        c_s[i] = cs[i]

    def cond(st):
        j, m = st
        return jnp.logical_and(j >= 0, m > SB_LOG_CUTOFF)

    def body(st):
        j, _ = st
        ws, cs, vts = cache_block(j, [c_s[i] for i in range(nh)])
        acc_s[...] += _dot_nt(ws, vts)
        for i in range(nh):
            c_s[i] = cs[i]
        return j - 1, _live(c_s, nh)

    lax.while_loop(cond, body, (jnp.int32(npast - 2), _live(c_s, nh)))
    o_ref[0] = acc_s[...].astype(BF16)


def _sb_sample(q, k, v, past_kt, past_vt, *, tk):
    bsz, tq, _ = q.shape
    past = past_kt.shape[3]
    width = PV_GROUP * SB_HEAD_DIM
    grid = (bsz, SB_HEADS // PV_GROUP)
    qspec = pl.BlockSpec((1, tq, width), lambda b, h: (b, 0, h))
    pspec = pl.BlockSpec((1, PV_GROUP, SB_HEAD_DIM, past), lambda b, h: (b, h, 0, 0))
    return pl.pallas_call(
        functools.partial(_sb_sample_kernel, tq=tq, tk=tk, npast=past // tk),
        grid=grid,
        in_specs=[qspec, qspec, qspec, pspec, pspec, _const_spec((tq, tq)), _const_spec((tk, tk))],
        out_specs=qspec,
        out_shape=jax.ShapeDtypeStruct((bsz, tq, SB_WIDTH), BF16),
        scratch_shapes=[pltpu.VMEM((tq, width), F32), pltpu.VMEM((PV_GROUP, tq, 1), F32)],
        compiler_params=pltpu.CompilerParams(
            dimension_semantics=("parallel", "parallel"), vmem_limit_bytes=VMEM_LIMIT),
        name="sb_sample",
    )(q, k, v, past_kt, past_vt, _suffix_tri(tq), _suffix_tri(tk))


def _post_kernel(x_ref, y_ref, o_ref, ga_ref, gb_ref, p_ref, wa_ref, wb_ref, wo_ref, gf_ref, wg_ref, wu_ref,
                 wd_ref, gp_ref, wpg_ref, wp_ref, gl_ref, out_ref, *, final_norm):
    branch_a = _dot(y_ref[...], wa_ref[...])
    branch_b = _dot(o_ref[...], wb_ref[...])
    merged = jax.nn.sigmoid(ga_ref[...]) * branch_a + jax.nn.sigmoid(gb_ref[...]) * branch_b
    x = x_ref[...] + _dot(merged.astype(BF16), wo_ref[...])
    hf = _rms(x, gf_ref[...]).astype(BF16)
    gate = _dot(hf, wg_ref[...])
    act = (gate * jax.nn.sigmoid(gate) * _dot(hf, wu_ref[...])).astype(BF16)
    x = x + _dot(act, wd_ref[...])
    g = jax.nn.sigmoid(_dot(_rms(x, gp_ref[...]).astype(BF16), wpg_ref[...]))
    x = x + g * _dot(p_ref[...].astype(BF16), wp_ref[...])
    if final_norm:
        x = _rms(x, gl_ref[...])
    out_ref[...] = x


def _post(x, y, o, ga, gb, p, wts, *, tm, final_norm):
    T = x.shape[0]

    def tok(n):
        return pl.BlockSpec((tm, n), lambda i: (i, 0))

    wspecs = [_const_spec(w.shape) for w in wts]
    return pl.pallas_call(
        functools.partial(_post_kernel, final_norm=final_norm),
        grid=(T // tm,),
        in_specs=[tok(D_MODEL), tok(M_INNER), tok(SB_WIDTH), tok(D_MODEL), tok(D_MODEL), tok(PLE_DIM)] + wspecs,
        out_specs=tok(D_MODEL),
        out_shape=jax.ShapeDtypeStruct((T, D_MODEL), F32),
        compiler_params=pltpu.CompilerParams(
            dimension_semantics=("parallel",), vmem_limit_bytes=VMEM_LIMIT),
        name="post",
    )(x, y, o, ga, gb, p, *wts)


def _pack_w_in(w_in):
    w = lax.optimization_barrier(w_in.astype(BF16))
    dt_cols = jnp.pad(w[:, OFF_DT:OFF_Q], ((0, 0), (0, DT_PAD - M_HEADS)))
    q_cols = (w_in[:, OFF_Q:OFF_K] * (SB_SCALE * LOG2_E)).astype(BF16)
    parts = [w[:, OFF_Z:OFF_DT], dt_cols, q_cols, w[:, OFF_K:IN_COLS]]
    return jnp.concatenate(parts, axis=1)


def _head_expand():
    r = lax.broadcasted_iota(jnp.int32, (DT_PAD, M_INNER), 0)
    c = lax.broadcasted_iota(jnp.int32, (DT_PAD, M_INNER), 1)
    return (c // M_HEAD_DIM == r).astype(BF16)


def _pad_heads(v):
    return jnp.pad(v.reshape(1, M_HEADS).astype(F32), ((0, 0), (0, DT_PAD - M_HEADS)))


def _layer(x, p, past_k, past_v, conv_hist, h0, prm, *, final_norm):
    bsz, L, _ = x.shape
    prompt = past_k is None
    if prompt:
        nb, tl = 1, min(L, TOKEN_TILE)
        conv_hist = jnp.zeros((bsz, CONV_W - 1, CONV_DIM), F32)
        h0 = jnp.zeros((bsz, M_HEADS, M_HEAD_DIM, M_STATE), F32)
    else:
        nb, tl = bsz, L
    zs, xs, bc, dt, q, k, v, new_k, new_v, ga, gb, new_conv = _inproj(
        x, conv_hist.astype(F32), prm["norm_mix_g"], prm["w_in"], prm["conv_w"], prm["conv_b"],
        nb=nb, tl=tl, kv_feature_major=prompt)
    if prompt:
        new_k, new_v = jnp.swapaxes(new_k, 2, 3), jnp.swapaxes(new_v, 2, 3)

    y, new_ssm = _ssd(xs, bc, zs, dt, h0.astype(F32), prm["dt_bias"], prm["a_log"], prm["d_skip"], prm["mnorm_g"],
                      prm["head_expand"], q=min(L, SSD_CHUNK))
    if prompt:
        o = _sb_prompt(q, k, v, tb=min(L, SB_BLOCK))
    else:
        o = _sb_sample(q, k, v, jnp.swapaxes(past_k.astype(F32), 2, 3), jnp.swapaxes(past_v.astype(F32), 2, 3),
                       tk=min(past_k.shape[2], SB_BLOCK))

    T = bsz * L
    flat = lambda a: a.reshape(T, a.shape[-1])
    x = _post(flat(x), flat(y), flat(o), flat(ga), flat(gb), flat(p), prm["post"], tm=min(T, TOKEN_TILE),
              final_norm=final_norm)
    return x.reshape(bsz, L, D_MODEL), new_k, new_v, new_conv, new_ssm


def _layer_params(i, norm_mix_g, w_in, conv_w, conv_b, dt_bias, a_log, d_skip, mnorm_g, w_a, w_b, w_out,
                  norm_ffn_g, w_gate, w_up, w_down, norm_ple_g, w_ple_gate, w_ple, final_norm_g):
    row = lambda v: v.reshape(1, -1).astype(F32)
    bf = lambda w: w.astype(BF16)
    return {
        "norm_mix_g": row(norm_mix_g[i]),
        "w_in": _pack_w_in(w_in[i]),
        "conv_w": conv_w[i].astype(F32),
        "conv_b": row(conv_b[i]),
        "dt_bias": _pad_heads(dt_bias[i]),
        "a_log": _pad_heads(a_log[i]),
        "d_skip": row(jnp.repeat(d_skip[i], M_HEAD_DIM)),
        "mnorm_g": row(mnorm_g[i]),
        "head_expand": _head_expand(),
        "post": (bf(w_a[i]), bf(w_b[i]), bf(w_out[i]), row(norm_ffn_g[i]), bf(w_gate[i]), bf(w_up[i]),
                 bf(w_down[i]), row(norm_ple_g[i]), bf(w_ple_gate[i]), bf(w_ple[i]), row(final_norm_g)),
    }


def kernel(x_prompt, x_sample, cache_k, cache_v, state_conv, state_ssm, p_prompt, p_sample, norm_mix_g, w_in,
           conv_w, conv_b, dt_bias, a_log, d_skip, mnorm_g, w_a, w_b, w_out, norm_ffn_g, w_gate, w_up, w_down,
           norm_ple_g, w_ple_gate, w_ple, final_norm_g):
    depth = w_in.shape[0]
    xp, xs = x_prompt, x_sample
    outs_p, outs_s = [], []
    for i in range(depth):
        prm = _layer_params(i, norm_mix_g, w_in, conv_w, conv_b, dt_bias, a_log, d_skip, mnorm_g, w_a, w_b,
                            w_out, norm_ffn_g, w_gate, w_up, w_down, norm_ple_g, w_ple_gate, w_ple, final_norm_g)
        last = i == depth - 1
        xp, *st_p = _layer(xp, p_prompt[i], None, None, None, None, prm, final_norm=last)
        xs, *st_s = _layer(xs, p_sample[i], cache_k[i], cache_v[i], state_conv[i], state_ssm[i], prm,
                           final_norm=last)
        outs_p.append(st_p)
        outs_s.append(st_s)
    stack = lambda outs, j: jnp.stack([o[j] for o in outs])
    return (xp, xs,
            stack(outs_p, 0), stack(outs_p, 1), stack(outs_p, 2), stack(outs_p, 3),
            stack(outs_s, 0), stack(outs_s, 1), stack(outs_s, 2), stack(outs_s, 3))
```

```python
import functools

import jax
import jax.numpy as jnp
from jax import lax
from jax.experimental import pallas as pl
from jax.experimental.pallas import tpu as pltpu

F32 = jnp.float32
BF16 = jnp.bfloat16

EPS = 1e-6
D_MODEL = 1024
M_HEAD_DIM = 64
M_HEADS = 16
M_GROUPS = 4
M_STATE = 128
M_INNER = M_HEADS * M_HEAD_DIM
CONV_W = 4
CONV_DIM = M_INNER + 2 * M_GROUPS * M_STATE
SB_HEADS = 16
SB_HEAD_DIM = 64
SB_WIDTH = SB_HEADS * SB_HEAD_DIM
SB_SCALE = SB_HEAD_DIM ** -0.5
PLE_DIM = 256
OFF_Z = 0
OFF_XBC = OFF_Z + M_INNER
OFF_DT = OFF_XBC + CONV_DIM
OFF_Q = OFF_DT + M_HEADS
OFF_K = OFF_Q + SB_WIDTH
OFF_V = OFF_K + SB_WIDTH
OFF_GA = OFF_V + SB_WIDTH
OFF_GB = OFF_GA + D_MODEL
IN_COLS = OFF_GB + D_MODEL

LANES = 128
SUBLANES = 8
DT_PAD = LANES
P_Z = 0
P_XBC = P_Z + M_INNER
P_DT = P_XBC + CONV_DIM
P_Q = P_DT + DT_PAD
P_K = P_Q + SB_WIDTH
P_V = P_K + SB_WIDTH
P_GA = P_V + SB_WIDTH
P_GB = P_GA + D_MODEL
P_COLS = P_GB + D_MODEL

VMEM_LIMIT = 56 * 1024 * 1024
LOG2_E = 1.4426950408889634
SB_LOG_CUTOFF = -150.0
TOKEN_TILE = 256
SB_BLOCK = 256
SSD_CHUNK = 256


def _const_spec(shape):
    nd = len(shape)
    return pl.BlockSpec(shape, lambda *_: (0,) * nd, pipeline_mode=pl.Buffered(1))


def _rms(x, g):
    ms = jnp.mean(x * x, axis=-1, keepdims=True)
    return x * lax.rsqrt(ms + EPS) * g


def _split2(x):
    hi = x.astype(BF16)
    lo = (x - hi.astype(F32)).astype(BF16)
    return hi, lo


def _split3(x):
    hi = x.astype(BF16)
    r = x - hi.astype(F32)
    mid = r.astype(BF16)
    lo = (r - mid.astype(F32)).astype(BF16)
    return hi, mid, lo


def _dot(a, b):
    return jnp.dot(a, b, preferred_element_type=F32)


def _dot_nt(a, b):
    return lax.dot_general(a, b, (((1,), (1,)), ((), ())), preferred_element_type=F32)


def _dot_tn(a, b):
    return lax.dot_general(a, b, (((0,), (0,)), ((), ())), preferred_element_type=F32)


def _split_heads(x2, first):
    zero = jnp.zeros_like(x2)
    return jnp.where(first, x2, zero), jnp.where(first, zero, x2)


HIST_ROW = SUBLANES - (CONV_W - 1)
BC_COLS = 2 * M_GROUPS * M_STATE
CONV_STRIP = 256
PROJ_PIECE = 512


def _inproj_kernel(x_ref, hist_ref, g_ref, w_ref, cw_ref, cb_ref, zs_ref, xs_ref, bc_ref, dt_ref, q_ref, k_ref,
                   v_ref, ko_ref, vo_ref, ga_ref, gb_ref, nconv_ref, xp_s, *, nb, tl, kv_feature_major):
    t = pl.program_id(1)
    tm = nb * tl
    x = x_ref[...].reshape(tm, D_MODEL)
    h = _rms(x, g_ref[...]).astype(BF16)

    def proj(lo, n):
        return _dot(h, w_ref[:, lo:lo + n])

    @pl.when(t == 0)
    def _():
        xp_s[:, HIST_ROW:SUBLANES, :] = hist_ref[...]

    def conv_cols(c0):
        cols = slice(c0, c0 + CONV_STRIP)
        xp_s[:, SUBLANES:SUBLANES + tl, cols] = proj(P_XBC + c0, CONV_STRIP).reshape(nb, tl, CONV_STRIP)
        xp = xp_s[:, :, cols]
        xc = cb_ref[:, cols] + xp[:, SUBLANES:SUBLANES + tl, :] * cw_ref[CONV_W - 1:CONV_W, cols]
        for k in range(1, CONV_W):
            tap = pltpu.roll(xp, k, axis=1)[:, SUBLANES:SUBLANES + tl, :]
            xc = xc + tap * cw_ref[CONV_W - 1 - k:CONV_W - k, cols]
        xc = xc * jax.nn.sigmoid(xc)
        if c0 < M_INNER:
            xs_ref[:, :, cols] = xc
        else:
            bc_ref[:, :, c0 - M_INNER:c0 - M_INNER + CONV_STRIP] = xc.astype(BF16)
        tail = xp_s[:, SUBLANES + tl - (CONV_W - 1):SUBLANES + tl, cols]
        nconv_ref[:, :, cols] = tail
        xp_s[:, HIST_ROW:SUBLANES, cols] = tail

    def half_cols(half):
        return slice(half * PROJ_PIECE, (half + 1) * PROJ_PIECE)

    def kv_proj(src, bf_ref, out_ref, half):
        cols = half_cols(half)
        kv = proj(src + cols.start, PROJ_PIECE)
        bf_ref[:, :, cols] = kv.astype(BF16).reshape(nb, tl, PROJ_PIECE)
        h0, nhd = cols.start // SB_HEAD_DIM, PROJ_PIECE // SB_HEAD_DIM
        if kv_feature_major:
            out_ref[0, h0:h0 + nhd] = kv.T.reshape(nhd, SB_HEAD_DIM, tl)
        else:
            for b in range(nb):
                for hd in range(nhd):
                    out_ref[b, h0 + hd] = kv[b * tl:(b + 1) * tl, hd * SB_HEAD_DIM:(hd + 1) * SB_HEAD_DIM]

    def z_proj(half):
        cols = half_cols(half)
        z = proj(P_Z + cols.start, PROJ_PIECE)
        zs_ref[:, :, cols] = (z * jax.nn.sigmoid(z)).reshape(nb, tl, PROJ_PIECE)

    def q_proj(half):
        cols = half_cols(half)
        if half == 0:
            dt_ref[...] = proj(P_DT, DT_PAD).reshape(nb, tl, DT_PAD)
        q_ref[:, :, cols] = (proj(P_Q + cols.start, PROJ_PIECE) * LOG2_E).astype(BF16).reshape(nb, tl, PROJ_PIECE)

    def gate_proj(src, out_ref, half):
        cols = half_cols(half)
        out_ref[:, :, cols] = proj(src + cols.start, PROJ_PIECE).reshape(nb, tl, PROJ_PIECE)

    others = []
    for fn in (z_proj, q_proj, functools.partial(kv_proj, P_K, k_ref, ko_ref),
               functools.partial(kv_proj, P_V, v_ref, vo_ref), functools.partial(gate_proj, P_GA, ga_ref),
               functools.partial(gate_proj, P_GB, gb_ref)):
        others += [functools.partial(fn, half) for half in range(D_MODEL // PROJ_PIECE)]
    strips = list(range(0, CONV_DIM, CONV_STRIP))
    done = 0
    for i, piece in enumerate(others):
        want = -(-(i + 1) * len(strips) // len(others))
        for c0 in strips[done:want]:
            conv_cols(c0)
        done = max(done, want)
        piece()


def _inproj(x, hist, g, w, cw, cb, *, nb, tl, kv_feature_major):
    bsz, L, _ = x.shape
    grid = (bsz // nb, L // tl)

    def tok(n):
        return pl.BlockSpec((nb, tl, n), lambda b, t: (b, t, 0))

    if kv_feature_major:
        assert nb == 1
        head_shape = (bsz, SB_HEADS, SB_HEAD_DIM, L)
        head = pl.BlockSpec((1, SB_HEADS, SB_HEAD_DIM, tl), lambda b, t: (b, 0, 0, t))
    else:
        head_shape = (bsz, SB_HEADS, L, SB_HEAD_DIM)
        head = pl.BlockSpec((nb, SB_HEADS, tl, SB_HEAD_DIM), lambda b, t: (b, 0, t, 0))
    hist_spec = pl.BlockSpec((nb, CONV_W - 1, CONV_DIM), lambda b, t: (b, 0, 0))
    out_shape = (
        jax.ShapeDtypeStruct((bsz, L, M_INNER), F32),
        jax.ShapeDtypeStruct((bsz, L, M_INNER), F32),
        jax.ShapeDtypeStruct((bsz, L, BC_COLS), BF16),
        jax.ShapeDtypeStruct((bsz, L, DT_PAD), F32),
        jax.ShapeDtypeStruct((bsz, L, SB_WIDTH), BF16),
        jax.ShapeDtypeStruct((bsz, L, SB_WIDTH), BF16),
        jax.ShapeDtypeStruct((bsz, L, SB_WIDTH), BF16),
        jax.ShapeDtypeStruct(head_shape, F32),
        jax.ShapeDtypeStruct(head_shape, F32),
        jax.ShapeDtypeStruct((bsz, L, D_MODEL), F32),
        jax.ShapeDtypeStruct((bsz, L, D_MODEL), F32),
        jax.ShapeDtypeStruct((bsz, CONV_W - 1, CONV_DIM), F32),
    )
    out_specs = (tok(M_INNER), tok(M_INNER), tok(BC_COLS), tok(DT_PAD), tok(SB_WIDTH), tok(SB_WIDTH), tok(SB_WIDTH),
                 head, head, tok(D_MODEL), tok(D_MODEL), hist_spec)
    return pl.pallas_call(
        functools.partial(_inproj_kernel, nb=nb, tl=tl, kv_feature_major=kv_feature_major),
        grid=grid,
        in_specs=[tok(D_MODEL), hist_spec, _const_spec((1, D_MODEL)), _const_spec((D_MODEL, P_COLS)),
                  _const_spec((CONV_W, CONV_DIM)), _const_spec((1, CONV_DIM))],
        out_specs=out_specs,
        out_shape=out_shape,
        scratch_shapes=[pltpu.VMEM((nb, SUBLANES + tl, CONV_DIM), F32)],
        compiler_params=pltpu.CompilerParams(
            dimension_semantics=("parallel", "arbitrary"), vmem_limit_bytes=VMEM_LIMIT),
        name="inproj",
    )(x, hist, g, w, cw, cb)


PAIRS = M_HEADS // 2
HEADS_PER_GROUP = M_HEADS // M_GROUPS
GROUP_COLS = HEADS_PER_GROUP * M_HEAD_DIM
SSD_BATCHES = 4


def _ssd_kernel(xs_ref, bc_ref, zs_ref, dt_ref, h0_ref, dtb_ref, alog_ref, dsk_ref, mg_ref, ex_ref,
                y_ref, nssm_ref, h_s, *, q, nbat):
    t = pl.program_id(1)
    nt = pl.num_programs(1)

    @pl.when(t == 0)
    def _():
        for b in range(nbat):
            for i in range(PAIRS):
                h_s[b, :, i * LANES:(i + 1) * LANES] = h0_ref[b, i].T

    ri = lax.broadcasted_iota(jnp.int32, (q, q), 0)
    ci = lax.broadcasted_iota(jnp.int32, (q, q), 1)
    causal = ri >= ci
    tri = causal.astype(BF16)
    first = lax.broadcasted_iota(jnp.int32, (q, LANES), 1) < M_HEAD_DIM
    ex = ex_ref[...]

    def head_rows(a):
        if q % LANES:
            a = jnp.concatenate([a, jnp.zeros((LANES - q, LANES), F32)], axis=0)
        return a.T[:, :q]

    for b in range(nbat):
        _ssd_chunk(b, xs_ref, bc_ref, zs_ref, dt_ref, dtb_ref, alog_ref, dsk_ref, mg_ref, y_ref, h_s,
                   q, causal, tri, first, ex, head_rows)

    @pl.when(t == nt - 1)
    def _():
        for b in range(nbat):
            for i in range(PAIRS):
                nssm_ref[b, i] = h_s[b, :, i * LANES:(i + 1) * LANES].T


def _ssd_chunk(b, xs_ref, bc_ref, zs_ref, dt_ref, dtb_ref, alog_ref, dsk_ref, mg_ref, y_ref, h_s,
               q, causal, tri, first, ex, head_rows):
    xs = xs_ref[b]
    bm = bc_ref[b, :, 0:M_GROUPS * M_STATE]
    cm = bc_ref[b, :, M_GROUPS * M_STATE:BC_COLS]
    xdt = dt_ref[b] + dtb_ref[...]
    dt = jnp.maximum(xdt, 0.0) + jnp.log1p(jnp.exp(-jnp.abs(xdt)))
    da = dt * (-jnp.exp(alog_ref[...]))

    acum = sum(_dot(tri, p) for p in _split3(da))
    tot = acum[q - 1:q, :]
    ey = jnp.exp(acum)
    ws = jnp.exp(tot - acum) * dt
    dc = jnp.broadcast_to(jnp.exp(tot), (SUBLANES, DT_PAD))
    stack_c = _dot(jnp.concatenate([ey, ws], axis=0).astype(BF16), ex)
    ey_c = stack_c[0:q]
    ws_c = stack_c[q:2 * q]
    dc_c = sum(_dot(p, ex) for p in _split2(dc))[0:1]
    acum2 = acum * LOG2_E
    acum2_t = head_rows(acum2)
    dt_t = head_rows(dt)

    def decay_weights(cb, hd, r0, nr, nc):
        acol = jnp.broadcast_to(acum2[r0:r0 + nr, hd:hd + 1], (nr, nc))
        arow = jnp.broadcast_to(acum2_t[hd:hd + 1, 0:nc], (nr, nc))
        dec = jnp.exp2(jnp.where(causal[r0:r0 + nr, 0:nc], acol - arow, -1e30))
        return (cb[r0:r0 + nr, 0:nc] * dec * jnp.broadcast_to(dt_t[hd:hd + 1, 0:nc], (nr, nc))).astype(BF16)

    bands = ((0, q // 2, q // 2), (q // 2, q // 2, q)) if q % (2 * LANES) == 0 else ((0, q, q),)

    xs_bf = xs.astype(BF16)
    xw_bf = (xs * ws_c).astype(BF16)
    h_prev = h_s[b]
    h_bf = h_prev.astype(BF16)

    ydiag, yoff, states = [], [], []
    for g in range(M_GROUPS):
        bg = bm[:, g * M_STATE:(g + 1) * M_STATE]
        cg = cm[:, g * M_STATE:(g + 1) * M_STATE]
        gcols = slice(g * GROUP_COLS, (g + 1) * GROUP_COLS)
        cb = _dot_nt(cg, bg)
        yoff.append(_dot(cg, h_bf[:, gcols]))
        states.append(_dot_tn(bg, xw_bf[:, gcols]))
        for pr in range(HEADS_PER_GROUP // 2):
            pair = g * (HEADS_PER_GROUP // 2) + pr
            xpair = xs_bf[:, pair * LANES:(pair + 1) * LANES]
            xe, xo = _split_heads(xpair, first)
            rows = []
            for r0, nr, nc in bands:
                ms = [decay_weights(cb, 2 * pair + r, r0, nr, nc) for r in range(2)]
                if q % LANES == 0:
                    rows.append(_dot(jnp.concatenate(ms, axis=1), jnp.concatenate([xe[0:nc], xo[0:nc]], axis=0)))
                else:
                    rows.append(jnp.where(first, _dot(ms[0], xpair), _dot(ms[1], xpair)))
            ydiag.append(rows[0] if len(rows) == 1 else jnp.concatenate(rows, axis=0))
    y = jnp.concatenate(ydiag, axis=1) + jnp.concatenate(yoff, axis=1) * ey_c + dsk_ref[...] * xs
    y = y * zs_ref[b]
    normed = []
    for g in range(M_GROUPS):
        yg = y[:, g * GROUP_COLS:(g + 1) * GROUP_COLS]
        normed.append(yg * lax.rsqrt(jnp.mean(yg * yg, axis=-1, keepdims=True) + EPS))
    y_ref[b] = (jnp.concatenate(normed, axis=1) * mg_ref[...]).astype(BF16)
    h_s[b] = h_prev * dc_c + jnp.concatenate(states, axis=1)


def _ssd(xs, bc, zs, dt, h0, dtb, alog, dsk, mg, ex, *, q):
    bsz, L, _ = xs.shape
    nbat = SSD_BATCHES if bsz % SSD_BATCHES == 0 else 1
    grid = (bsz // nbat, L // q)
    h0p = h0.reshape(bsz, PAIRS, LANES, M_STATE)

    def tok(n):
        return pl.BlockSpec((nbat, q, n), lambda b, t: (b, t, 0))

    per_b = pl.BlockSpec((nbat, PAIRS, LANES, M_STATE), lambda b, t: (b, 0, 0, 0))
    y, nssm = pl.pallas_call(
        functools.partial(_ssd_kernel, q=q, nbat=nbat),
        grid=grid,
        in_specs=[tok(M_INNER), tok(BC_COLS), tok(M_INNER), tok(DT_PAD), per_b,
                  _const_spec((1, DT_PAD)), _const_spec((1, DT_PAD)), _const_spec((1, M_INNER)),
                  _const_spec((1, M_INNER)), _const_spec((DT_PAD, M_INNER))],
        out_specs=(tok(M_INNER), per_b),
        out_shape=(jax.ShapeDtypeStruct((bsz, L, M_INNER), BF16),
                   jax.ShapeDtypeStruct((bsz, PAIRS, LANES, M_STATE), F32)),
        scratch_shapes=[pltpu.VMEM((nbat, M_STATE, M_INNER), F32)],
        compiler_params=pltpu.CompilerParams(
            dimension_semantics=("parallel", "arbitrary"), vmem_limit_bytes=VMEM_LIMIT),
        name="ssd",
    )(xs, bc, zs, dt, h0p, dtb, alog, dsk, mg, ex)
    return y, nssm.reshape(bsz, M_HEADS, M_HEAD_DIM, M_STATE)


def _sb_weights(u, tri, carry, mask):
    drop = jnp.maximum(u, 0.0) + jnp.log2(1.0 + jnp.exp2(-jnp.abs(u)))
    if mask is not None:
        drop = jnp.where(mask, drop, 0.0)
    cs = _dot(drop.astype(BF16), tri)
    w = jnp.exp2(u - cs + carry)
    if mask is not None:
        w = jnp.where(mask, w, 0.0)
    return w.astype(BF16), carry - cs[:, 0:1]


def _suffix_tri(n):
    r = lax.broadcasted_iota(jnp.int32, (n, n), 0)
    c = lax.broadcasted_iota(jnp.int32, (n, n), 1)
    return (r >= c).astype(BF16)


def _strictly_before(n):
    return lax.broadcasted_iota(jnp.int32, (n, n), 1) < lax.broadcasted_iota(jnp.int32, (n, n), 0)


def _live(c_s, nh):
    return jnp.max(functools.reduce(jnp.maximum, [c_s[i] for i in range(nh)]))


SB_STEP_HEADS = 16
PV_GROUP = 4


def _sb_prompt_kernel(q_ref, k_ref, v_ref, tri_ref, o_ref, acc_s, c_s, *, tb):
    qi = pl.program_id(2)
    nh = SB_STEP_HEADS
    width = nh * SB_HEAD_DIM
    first = lax.broadcasted_iota(jnp.int32, (tb, LANES), 1) < SB_HEAD_DIM
    lane_head = lax.broadcasted_iota(jnp.int32, (tb, PV_GROUP * SB_HEAD_DIM), 1) // SB_HEAD_DIM
    tri = tri_ref[...]
    diag_mask = _strictly_before(tb)
    qh = []
    for p in range(nh // 2):
        qh += list(_split_heads(q_ref[0, :, p * LANES:(p + 1) * LANES], first))

    def weights(j, mask, carries):
        rows = pl.ds(pl.multiple_of(j * tb, tb), tb)
        ws, cs = [], []
        for i in range(nh):
            k2 = k_ref[0, rows, (i // 2) * LANES:(i // 2 + 1) * LANES]
            w, c = _sb_weights(_dot_nt(qh[i], k2), tri, carries[i], mask)
            ws.append(w)
            cs.append(c)
        return ws, cs

    def pv(blocks):
        outs = []
        for g in range(nh // PV_GROUP):
            lanes = slice(g * PV_GROUP * SB_HEAD_DIM, (g + 1) * PV_GROUP * SB_HEAD_DIM)
            lhs, rhs = [], []
            for ws, j in blocks:
                v = v_ref[0, pl.ds(pl.multiple_of(j * tb, tb), tb), lanes]
                zero = jnp.zeros_like(v)
                for i in range(PV_GROUP):
                    lhs.append(ws[g * PV_GROUP + i])
                    rhs.append(jnp.where(lane_head == i, v, zero))
            outs.append(_dot(jnp.concatenate(lhs, axis=1), jnp.concatenate(rhs, axis=0)))
        return outs[0] if len(outs) == 1 else jnp.concatenate(outs, axis=1)

    def keep(cs):
        for i in range(nh):
            c_s[i] = cs[i]

    zero = [jnp.zeros((tb, 1), F32)] * nh

    @pl.when(qi == 0)
    def _():
        ws, cs = weights(0, diag_mask, zero)
        acc_s[...] = pv([(ws, 0)])
        keep(cs)

    @pl.when(qi > 0)
    def _():
        ws_d, cs = weights(qi, diag_mask, zero)
        ws_p, cs = weights(qi - 1, None, cs)
        acc_s[...] = pv([(ws_d, qi), (ws_p, qi - 1)])
        keep(cs)

    def cond(st):
        j, m = st
        return jnp.logical_and(j >= 0, m > SB_LOG_CUTOFF)

    def body(st):
        j, _ = st
        ws, cs = weights(j, None, [c_s[i] for i in range(nh)])
        acc_s[...] += pv([(ws, j)])
        keep(cs)
        return j - 1, _live(c_s, nh)

    lax.while_loop(cond, body, (qi - 2, _live(c_s, nh)))
    o_ref[0] = acc_s[...].astype(BF16)


def _sb_prompt(q, k, v, *, tb):
    bsz, L, _ = q.shape
    width = SB_STEP_HEADS * SB_HEAD_DIM
    grid = (bsz, SB_HEADS // SB_STEP_HEADS, L // tb)
    qspec = pl.BlockSpec((1, tb, width), lambda b, h, i: (b, i, h))
    kspec = pl.BlockSpec((1, L, width), lambda b, h, i: (b, 0, h))
    return pl.pallas_call(
        functools.partial(_sb_prompt_kernel, tb=tb),
        grid=grid,
        in_specs=[qspec, kspec, kspec, _const_spec((tb, tb))],
        out_specs=qspec,
        out_shape=jax.ShapeDtypeStruct((bsz, L, SB_WIDTH), BF16),
        scratch_shapes=[pltpu.VMEM((tb, width), F32), pltpu.VMEM((SB_STEP_HEADS, tb, 1), F32)],
        compiler_params=pltpu.CompilerParams(
            dimension_semantics=("parallel", "parallel", "arbitrary"), vmem_limit_bytes=VMEM_LIMIT),
        name="sb_prompt",
    )(q, k, v, _suffix_tri(tb))


def _sb_sample_kernel(q_ref, k_ref, v_ref, pk_ref, pv_ref, tri_d_ref, tri_p_ref, o_ref, acc_s, c_s, *, tq, tk,
                      npast):
    nh = PV_GROUP
    width = nh * SB_HEAD_DIM
    q2 = q_ref[0]
    first = lax.broadcasted_iota(jnp.int32, (tq, LANES), 1) < SB_HEAD_DIM
    lane_head = lax.broadcasted_iota(jnp.int32, (tq, width), 1) // SB_HEAD_DIM
    row_head = lax.broadcasted_iota(jnp.int32, (width, tk), 0) // SB_HEAD_DIM
    tri_p = tri_p_ref[...]
    qh = []
    for p in range(nh // 2):
        qh += list(_split_heads(q2[:, p * LANES:(p + 1) * LANES], first))

    def cache_block(j, carries):
        cols = pl.ds(pl.multiple_of(j * tk, tk), tk)
        ws, cs = [], []
        for i in range(nh):
            qs = q2[:, i * SB_HEAD_DIM:(i + 1) * SB_HEAD_DIM]
            w, c = _sb_weights(_dot(qs, pk_ref[0, i, :, cols].astype(BF16)), tri_p, carries[i], None)
            ws.append(w)
            cs.append(c)
        vt = pv_ref[0, :, :, cols].astype(BF16).reshape(width, tk)
        zero = jnp.zeros_like(vt)
        vts = jnp.concatenate([jnp.where(row_head == i, vt, zero) for i in range(nh)], axis=1)
        return jnp.concatenate(ws, axis=1), cs, vts

    ws_d, cs = [], []
    for i in range(nh):
        k2 = k_ref[0, :, (i // 2) * LANES:(i // 2 + 1) * LANES]
        w, c = _sb_weights(_dot_nt(qh[i], k2), tri_d_ref[...], jnp.zeros((tq, 1), F32), _strictly_before(tq))
        ws_d.append(w)
        cs.append(c)
    vnew = v_ref[0]
    vals_d = jnp.concatenate([jnp.where(lane_head == i, vnew, jnp.zeros_like(vnew)) for i in range(nh)], axis=0)
    ws_p, cs, vts = cache_block(npast - 1, cs)
    acc_s[...] = _dot(jnp.concatenate(ws_d, axis=1), vals_d) + _dot_nt(ws_p, vts)
    for i in range(nh):
        c_s[i] = cs[i]

    def cond(st):
        j, m = st
        return jnp.logical_and(j >= 0, m > SB_LOG_CUTOFF)

    def body(st):
        j, _ = st
        ws, cs, vts = cache_block(j, [c_s[i] for i in range(nh)])
        acc_s[...] += _dot_nt(ws, vts)
        for i in range(nh):
            c_s[i] = cs[i]
        return j - 1, _live(c_s, nh)

    lax.while_loop(cond, body, (jnp.int32(npast - 2), _live(c_s, nh)))
    o_ref[0] = acc_s[...].astype(BF16)


def _sb_sample(q, k, v, past_kt, past_vt, *, tk):
    bsz, tq, _ = q.shape
    past = past_kt.shape[3]
    width = PV_GROUP * SB_HEAD_DIM
    grid = (bsz, SB_HEADS // PV_GROUP)
    qspec = pl.BlockSpec((1, tq, width), lambda b, h: (b, 0, h))
    pspec = pl.BlockSpec((1, PV_GROUP, SB_HEAD_DIM, past), lambda b, h: (b, h, 0, 0))
    return pl.pallas_call(
        functools.partial(_sb_sample_kernel, tq=tq, tk=tk, npast=past // tk),
        grid=grid,
        in_specs=[qspec, qspec, qspec, pspec, pspec, _const_spec((tq, tq)), _const_spec((tk, tk))],
        out_specs=qspec,
        out_shape=jax.ShapeDtypeStruct((bsz, tq, SB_WIDTH), BF16),
        scratch_shapes=[pltpu.VMEM((tq, width), F32), pltpu.VMEM((PV_GROUP, tq, 1), F32)],
        compiler_params=pltpu.CompilerParams(
            dimension_semantics=("parallel", "parallel"), vmem_limit_bytes=VMEM_LIMIT),
        name="sb_sample",
    )(q, k, v, past_kt, past_vt, _suffix_tri(tq), _suffix_tri(tk))


def _post_kernel(x_ref, y_ref, o_ref, ga_ref, gb_ref, p_ref, wa_ref, wb_ref, wo_ref, gf_ref, wg_ref, wu_ref,
                 wd_ref, gp_ref, wpg_ref, wp_ref, gl_ref, out_ref, *, final_norm):
    branch_a = _dot(y_ref[...], wa_ref[...])
    branch_b = _dot(o_ref[...], wb_ref[...])
    merged = jax.nn.sigmoid(ga_ref[...]) * branch_a + jax.nn.sigmoid(gb_ref[...]) * branch_b
    x = x_ref[...] + _dot(merged.astype(BF16), wo_ref[...])
    hf = _rms(x, gf_ref[...]).astype(BF16)
    gate = _dot(hf, wg_ref[...])
    act = (gate * jax.nn.sigmoid(gate) * _dot(hf, wu_ref[...])).astype(BF16)
    x = x + _dot(act, wd_ref[...])
    g = jax.nn.sigmoid(_dot(_rms(x, gp_ref[...]).astype(BF16), wpg_ref[...]))
    x = x + g * _dot(p_ref[...].astype(BF16), wp_ref[...])
    if final_norm:
        x = _rms(x, gl_ref[...])
    out_ref[...] = x


def _post(x, y, o, ga, gb, p, wts, *, tm, final_norm):
    T = x.shape[0]

    def tok(n):
        return pl.BlockSpec((tm, n), lambda i: (i, 0))

    wspecs = [_const_spec(w.shape) for w in wts]
    return pl.pallas_call(
        functools.partial(_post_kernel, final_norm=final_norm),
        grid=(T // tm,),
        in_specs=[tok(D_MODEL), tok(M_INNER), tok(SB_WIDTH), tok(D_MODEL), tok(D_MODEL), tok(PLE_DIM)] + wspecs,
        out_specs=tok(D_MODEL),
        out_shape=jax.ShapeDtypeStruct((T, D_MODEL), F32),
        compiler_params=pltpu.CompilerParams(
            dimension_semantics=("parallel",), vmem_limit_bytes=VMEM_LIMIT),
        name="post",
    )(x, y, o, ga, gb, p, *wts)


def _pack_w_in(w_in):
    w = lax.optimization_barrier(w_in.astype(BF16))
    dt_cols = jnp.pad(w[:, OFF_DT:OFF_Q], ((0, 0), (0, DT_PAD - M_HEADS)))
    parts = [w[:, OFF_Z:OFF_DT], dt_cols, w[:, OFF_Q:OFF_K] * SB_SCALE, w[:, OFF_K:IN_COLS]]
    return jnp.concatenate(parts, axis=1)


def _head_expand():
    r = lax.broadcasted_iota(jnp.int32, (DT_PAD, M_INNER), 0)
    c = lax.broadcasted_iota(jnp.int32, (DT_PAD, M_INNER), 1)
    return (c // M_HEAD_DIM == r).astype(BF16)


def _pad_heads(v):
    return jnp.pad(v.reshape(1, M_HEADS).astype(F32), ((0, 0), (0, DT_PAD - M_HEADS)))


def _layer(x, p, past_k, past_v, conv_hist, h0, prm, *, final_norm):
    bsz, L, _ = x.shape
    prompt = past_k is None
    if prompt:
        nb, tl = 1, min(L, TOKEN_TILE)
        conv_hist = jnp.zeros((bsz, CONV_W - 1, CONV_DIM), F32)
        h0 = jnp.zeros((bsz, M_HEADS, M_HEAD_DIM, M_STATE), F32)
    else:
        nb, tl = bsz, L
    zs, xs, bc, dt, q, k, v, new_k, new_v, ga, gb, new_conv = _inproj(
        x, conv_hist.astype(F32), prm["norm_mix_g"], prm["w_in"], prm["conv_w"], prm["conv_b"],
        nb=nb, tl=tl, kv_feature_major=prompt)
    if prompt:
        new_k, new_v = jnp.swapaxes(new_k, 2, 3), jnp.swapaxes(new_v, 2, 3)

    y, new_ssm = _ssd(xs, bc, zs, dt, h0.astype(F32), prm["dt_bias"], prm["a_log"], prm["d_skip"], prm["mnorm_g"],
                      prm["head_expand"], q=min(L, SSD_CHUNK))
    if prompt:
        o = _sb_prompt(q, k, v, tb=min(L, SB_BLOCK))
    else:
        o = _sb_sample(q, k, v, jnp.swapaxes(past_k.astype(F32), 2, 3), jnp.swapaxes(past_v.astype(F32), 2, 3),
                       tk=min(past_k.shape[2], SB_BLOCK))

    T = bsz * L
    flat = lambda a: a.reshape(T, a.shape[-1])
    x = _post(flat(x), flat(y), flat(o), flat(ga), flat(gb), flat(p), prm["post"], tm=min(T, TOKEN_TILE),
              final_norm=final_norm)
    return x.reshape(bsz, L, D_MODEL), new_k, new_v, new_conv, new_ssm


def _layer_params(i, norm_mix_g, w_in, conv_w, conv_b, dt_bias, a_log, d_skip, mnorm_g, w_a, w_b, w_out,
                  norm_ffn_g, w_gate, w_up, w_down, norm_ple_g, w_ple_gate, w_ple, final_norm_g):
    row = lambda v: v.reshape(1, -1).astype(F32)
    bf = lambda w: w.astype(BF16)
    return {
        "norm_mix_g": row(norm_mix_g[i]),
        "w_in": _pack_w_in(w_in[i]),
        "conv_w": conv_w[i].astype(F32),
        "conv_b": row(conv_b[i]),
        "dt_bias": _pad_heads(dt_bias[i]),
        "a_log": _pad_heads(a_log[i]),
        "d_skip": row(jnp.repeat(d_skip[i], M_HEAD_DIM)),
        "mnorm_g": row(mnorm_g[i]),
        "head_expand": _head_expand(),
        "post": (bf(w_a[i]), bf(w_b[i]), bf(w_out[i]), row(norm_ffn_g[i]), bf(w_gate[i]), bf(w_up[i]),
                 bf(w_down[i]), row(norm_ple_g[i]), bf(w_ple_gate[i]), bf(w_ple[i]), row(final_norm_g)),
    }


def kernel(x_prompt, x_sample, cache_k, cache_v, state_conv, state_ssm, p_prompt, p_sample, norm_mix_g, w_in,
           conv_w, conv_b, dt_bias, a_log, d_skip, mnorm_g, w_a, w_b, w_out, norm_ffn_g, w_gate, w_up, w_down,
           norm_ple_g, w_ple_gate, w_ple, final_norm_g):
    depth = w_in.shape[0]
    xp, xs = x_prompt, x_sample
    outs_p, outs_s = [], []
    for i in range(depth):
        prm = _layer_params(i, norm_mix_g, w_in, conv_w, conv_b, dt_bias, a_log, d_skip, mnorm_g, w_a, w_b,
                            w_out, norm_ffn_g, w_gate, w_up, w_down, norm_ple_g, w_ple_gate, w_ple, final_norm_g)
        last = i == depth - 1
        xp, *st_p = _layer(xp, p_prompt[i], None, None, None, None, prm, final_norm=last)
        xs, *st_s = _layer(xs, p_sample[i], cache_k[i], cache_v[i], state_conv[i], state_ssm[i], prm,
                           final_norm=last)
        outs_p.append(st_p)
        outs_s.append(st_s)
    stack = lambda outs, j: jnp.stack([o[j] for o in outs])
    return (xp, xs,
            stack(outs_p, 0), stack(outs_p, 1), stack(outs_p, 2), stack(outs_p, 3),
            stack(outs_s, 0), stack(outs_s, 1), stack(outs_s, 2), stack(outs_s, 3))
```

```python
import functools

import jax
import jax.numpy as jnp
from jax import lax
from jax.experimental import pallas as pl
from jax.experimental.pallas import tpu as pltpu

F32 = jnp.float32
BF16 = jnp.bfloat16

EPS = 1e-6
D_MODEL = 1024
M_HEAD_DIM = 64
M_HEADS = 16
M_GROUPS = 4
M_STATE = 128
M_INNER = M_HEADS * M_HEAD_DIM
CONV_W = 4
CONV_DIM = M_INNER + 2 * M_GROUPS * M_STATE
SB_HEADS = 16
SB_HEAD_DIM = 64
SB_WIDTH = SB_HEADS * SB_HEAD_DIM
SB_SCALE = SB_HEAD_DIM ** -0.5
PLE_DIM = 256
OFF_Z = 0
OFF_XBC = OFF_Z + M_INNER
OFF_DT = OFF_XBC + CONV_DIM
OFF_Q = OFF_DT + M_HEADS
OFF_K = OFF_Q + SB_WIDTH
OFF_V = OFF_K + SB_WIDTH
OFF_GA = OFF_V + SB_WIDTH
OFF_GB = OFF_GA + D_MODEL
IN_COLS = OFF_GB + D_MODEL

LANES = 128
SUBLANES = 8
DT_PAD = LANES
P_Z = 0
P_XBC = P_Z + M_INNER
P_DT = P_XBC + CONV_DIM
P_Q = P_DT + DT_PAD
P_K = P_Q + SB_WIDTH
P_V = P_K + SB_WIDTH
P_GA = P_V + SB_WIDTH
P_GB = P_GA + D_MODEL
P_COLS = P_GB + D_MODEL

VMEM_LIMIT = 56 * 1024 * 1024
LOG2_E = 1.4426950408889634
SB_LOG_CUTOFF = -150.0
TOKEN_TILE = 256
SB_BLOCK = 256
SSD_CHUNK = 256


def _const_spec(shape):
    nd = len(shape)
    return pl.BlockSpec(shape, lambda *_: (0,) * nd, pipeline_mode=pl.Buffered(1))


def _rms(x, g):
    ms = jnp.mean(x * x, axis=-1, keepdims=True)
    return x * lax.rsqrt(ms + EPS) * g


def _split2(x):
    hi = x.astype(BF16)
    lo = (x - hi.astype(F32)).astype(BF16)
    return hi, lo


def _split3(x):
    hi = x.astype(BF16)
    r = x - hi.astype(F32)
    mid = r.astype(BF16)
    lo = (r - mid.astype(F32)).astype(BF16)
    return hi, mid, lo


def _dot(a, b):
    return jnp.dot(a, b, preferred_element_type=F32)


def _dot_nt(a, b):
    return lax.dot_general(a, b, (((1,), (1,)), ((), ())), preferred_element_type=F32)


def _dot_tn(a, b):
    return lax.dot_general(a, b, (((0,), (0,)), ((), ())), preferred_element_type=F32)


def _split_heads(x2, first):
    zero = jnp.zeros_like(x2)
    return jnp.where(first, x2, zero), jnp.where(first, zero, x2)


HIST_ROW = SUBLANES - (CONV_W - 1)
BC_COLS = 2 * M_GROUPS * M_STATE
CONV_STRIP = 256
PROJ_PIECE = 512


def _inproj_kernel(x_ref, hist_ref, g_ref, w_ref, cw_ref, cb_ref, zs_ref, xs_ref, bc_ref, dt_ref, q_ref, k_ref,
                   v_ref, ko_ref, vo_ref, ga_ref, gb_ref, nconv_ref, xp_s, *, nb, tl, kv_feature_major):
    t = pl.program_id(1)
    tm = nb * tl
    x = x_ref[...].reshape(tm, D_MODEL)
    h = _rms(x, g_ref[...]).astype(BF16)

    def proj(lo, n):
        return _dot(h, w_ref[:, lo:lo + n])

    @pl.when(t == 0)
    def _():
        xp_s[:, HIST_ROW:SUBLANES, :] = hist_ref[...]

    def conv_cols(c0):
        cols = slice(c0, c0 + CONV_STRIP)
        xp_s[:, SUBLANES:SUBLANES + tl, cols] = proj(P_XBC + c0, CONV_STRIP).reshape(nb, tl, CONV_STRIP)
        xp = xp_s[:, :, cols]
        xc = cb_ref[:, cols] + xp[:, SUBLANES:SUBLANES + tl, :] * cw_ref[CONV_W - 1:CONV_W, cols]
        for k in range(1, CONV_W):
            tap = pltpu.roll(xp, k, axis=1)[:, SUBLANES:SUBLANES + tl, :]
            xc = xc + tap * cw_ref[CONV_W - 1 - k:CONV_W - k, cols]
        xc = xc * jax.nn.sigmoid(xc)
        if c0 < M_INNER:
            xs_ref[:, :, cols] = xc
        else:
            bc_ref[:, :, c0 - M_INNER:c0 - M_INNER + CONV_STRIP] = xc.astype(BF16)
        tail = xp_s[:, SUBLANES + tl - (CONV_W - 1):SUBLANES + tl, cols]
        nconv_ref[:, :, cols] = tail
        xp_s[:, HIST_ROW:SUBLANES, cols] = tail

    def half_cols(half):
        return slice(half * PROJ_PIECE, (half + 1) * PROJ_PIECE)

    def kv_proj(src, bf_ref, out_ref, half):
        cols = half_cols(half)
        kv = proj(src + cols.start, PROJ_PIECE)
        bf_ref[:, :, cols] = kv.astype(BF16).reshape(nb, tl, PROJ_PIECE)
        h0, nhd = cols.start // SB_HEAD_DIM, PROJ_PIECE // SB_HEAD_DIM
        if kv_feature_major:
            out_ref[0, h0:h0 + nhd] = kv.T.reshape(nhd, SB_HEAD_DIM, tl)
        else:
            for b in range(nb):
                for hd in range(nhd):
                    out_ref[b, h0 + hd] = kv[b * tl:(b + 1) * tl, hd * SB_HEAD_DIM:(hd + 1) * SB_HEAD_DIM]

    def z_proj(half):
        cols = half_cols(half)
        z = proj(P_Z + cols.start, PROJ_PIECE)
        zs_ref[:, :, cols] = (z * jax.nn.sigmoid(z)).reshape(nb, tl, PROJ_PIECE)

    def q_proj(half):
        cols = half_cols(half)
        if half == 0:
            dt_ref[...] = proj(P_DT, DT_PAD).reshape(nb, tl, DT_PAD)
        q_ref[:, :, cols] = (proj(P_Q + cols.start, PROJ_PIECE) * LOG2_E).astype(BF16).reshape(nb, tl, PROJ_PIECE)

    def gate_proj(src, out_ref, half):
        cols = half_cols(half)
        out_ref[:, :, cols] = proj(src + cols.start, PROJ_PIECE).reshape(nb, tl, PROJ_PIECE)

    others = []
    for fn in (z_proj, q_proj, functools.partial(kv_proj, P_K, k_ref, ko_ref),
               functools.partial(kv_proj, P_V, v_ref, vo_ref), functools.partial(gate_proj, P_GA, ga_ref),
               functools.partial(gate_proj, P_GB, gb_ref)):
        others += [functools.partial(fn, half) for half in range(D_MODEL // PROJ_PIECE)]
    strips = list(range(0, CONV_DIM, CONV_STRIP))
    done = 0
    for i, piece in enumerate(others):
        want = -(-(i + 1) * len(strips) // len(others))
        for c0 in strips[done:want]:
            conv_cols(c0)
        done = max(done, want)
        piece()


def _inproj(x, hist, g, w, cw, cb, *, nb, tl, kv_feature_major):
    bsz, L, _ = x.shape
    grid = (bsz // nb, L // tl)

    def tok(n):
        return pl.BlockSpec((nb, tl, n), lambda b, t: (b, t, 0))

    if kv_feature_major:
        assert nb == 1
        head_shape = (bsz, SB_HEADS, SB_HEAD_DIM, L)
        head = pl.BlockSpec((1, SB_HEADS, SB_HEAD_DIM, tl), lambda b, t: (b, 0, 0, t))
    else:
        head_shape = (bsz, SB_HEADS, L, SB_HEAD_DIM)
        head = pl.BlockSpec((nb, SB_HEADS, tl, SB_HEAD_DIM), lambda b, t: (b, 0, t, 0))
    hist_spec = pl.BlockSpec((nb, CONV_W - 1, CONV_DIM), lambda b, t: (b, 0, 0))
    out_shape = (
        jax.ShapeDtypeStruct((bsz, L, M_INNER), F32),
        jax.ShapeDtypeStruct((bsz, L, M_INNER), F32),
        jax.ShapeDtypeStruct((bsz, L, BC_COLS), BF16),
        jax.ShapeDtypeStruct((bsz, L, DT_PAD), F32),
        jax.ShapeDtypeStruct((bsz, L, SB_WIDTH), BF16),
        jax.ShapeDtypeStruct((bsz, L, SB_WIDTH), BF16),
        jax.ShapeDtypeStruct((bsz, L, SB_WIDTH), BF16),
        jax.ShapeDtypeStruct(head_shape, F32),
        jax.ShapeDtypeStruct(head_shape, F32),
        jax.ShapeDtypeStruct((bsz, L, D_MODEL), F32),
        jax.ShapeDtypeStruct((bsz, L, D_MODEL), F32),
        jax.ShapeDtypeStruct((bsz, CONV_W - 1, CONV_DIM), F32),
    )
    out_specs = (tok(M_INNER), tok(M_INNER), tok(BC_COLS), tok(DT_PAD), tok(SB_WIDTH), tok(SB_WIDTH), tok(SB_WIDTH),
                 head, head, tok(D_MODEL), tok(D_MODEL), hist_spec)
    return pl.pallas_call(
        functools.partial(_inproj_kernel, nb=nb, tl=tl, kv_feature_major=kv_feature_major),
        grid=grid,
        in_specs=[tok(D_MODEL), hist_spec, _const_spec((1, D_MODEL)), _const_spec((D_MODEL, P_COLS)),
                  _const_spec((CONV_W, CONV_DIM)), _const_spec((1, CONV_DIM))],
        out_specs=out_specs,
        out_shape=out_shape,
        scratch_shapes=[pltpu.VMEM((nb, SUBLANES + tl, CONV_DIM), F32)],
        compiler_params=pltpu.CompilerParams(
            dimension_semantics=("parallel", "arbitrary"), vmem_limit_bytes=VMEM_LIMIT),
        name="inproj",
    )(x, hist, g, w, cw, cb)


PAIRS = M_HEADS // 2
HEADS_PER_GROUP = M_HEADS // M_GROUPS
GROUP_COLS = HEADS_PER_GROUP * M_HEAD_DIM
SSD_BATCHES = 4


def _ssd_kernel(xs_ref, bc_ref, zs_ref, dt_ref, h0_ref, dtb_ref, alog_ref, dsk_ref, mg_ref, ex_ref,
                y_ref, nssm_ref, h_s, *, q, nbat):
    t = pl.program_id(1)
    nt = pl.num_programs(1)

    @pl.when(t == 0)
    def _():
        for b in range(nbat):
            for i in range(PAIRS):
                h_s[b, :, i * LANES:(i + 1) * LANES] = h0_ref[b, i].T

    ri = lax.broadcasted_iota(jnp.int32, (q, q), 0)
    ci = lax.broadcasted_iota(jnp.int32, (q, q), 1)
    causal = ri >= ci
    tri = causal.astype(BF16)
    first = lax.broadcasted_iota(jnp.int32, (q, LANES), 1) < M_HEAD_DIM
    ex = ex_ref[...]

    def head_rows(a):
        if q % LANES:
            a = jnp.concatenate([a, jnp.zeros((LANES - q, LANES), F32)], axis=0)
        return a.T[:, :q]

    for b in range(nbat):
        _ssd_chunk(b, xs_ref, bc_ref, zs_ref, dt_ref, dtb_ref, alog_ref, dsk_ref, mg_ref, y_ref, h_s,
                   q, causal, tri, first, ex, head_rows)

    @pl.when(t == nt - 1)
    def _():
        for b in range(nbat):
            for i in range(PAIRS):
                nssm_ref[b, i] = h_s[b, :, i * LANES:(i + 1) * LANES].T


def _ssd_chunk(b, xs_ref, bc_ref, zs_ref, dt_ref, dtb_ref, alog_ref, dsk_ref, mg_ref, y_ref, h_s,
               q, causal, tri, first, ex, head_rows):
    xs = xs_ref[b]
    bm = bc_ref[b, :, 0:M_GROUPS * M_STATE]
    cm = bc_ref[b, :, M_GROUPS * M_STATE:BC_COLS]
    xdt = dt_ref[b] + dtb_ref[...]
    dt = jnp.maximum(xdt, 0.0) + jnp.log1p(jnp.exp(-jnp.abs(xdt)))
    da = dt * (-jnp.exp(alog_ref[...]))

    acum = sum(_dot(tri, p) for p in _split3(da))
    tot = acum[q - 1:q, :]
    ey = jnp.exp(acum)
    ws = jnp.exp(tot - acum) * dt
    dc = jnp.broadcast_to(jnp.exp(tot), (SUBLANES, DT_PAD))
    stack_c = _dot(jnp.concatenate([ey, ws], axis=0).astype(BF16), ex)
    ey_c = stack_c[0:q]
    ws_c = stack_c[q:2 * q]
    dc_c = sum(_dot(p, ex) for p in _split2(dc))[0:1]
    acum2 = acum * LOG2_E
    acum2_t = head_rows(acum2)
    dt_t = head_rows(dt)

    def decay_weights(cb, hd, r0, nr, nc):
        acol = jnp.broadcast_to(acum2[r0:r0 + nr, hd:hd + 1], (nr, nc))
        arow = jnp.broadcast_to(acum2_t[hd:hd + 1, 0:nc], (nr, nc))
        dec = jnp.exp2(jnp.where(causal[r0:r0 + nr, 0:nc], acol - arow, -1e30))
        return (cb[r0:r0 + nr, 0:nc] * dec * jnp.broadcast_to(dt_t[hd:hd + 1, 0:nc], (nr, nc))).astype(BF16)

    bands = ((0, q // 2, q // 2), (q // 2, q // 2, q)) if q % (2 * LANES) == 0 else ((0, q, q),)

    xs_bf = xs.astype(BF16)
    xw_bf = (xs * ws_c).astype(BF16)
    h_prev = h_s[b]
    h_bf = h_prev.astype(BF16)

    ydiag, yoff, states = [], [], []
    for g in range(M_GROUPS):
        bg = bm[:, g * M_STATE:(g + 1) * M_STATE]
        cg = cm[:, g * M_STATE:(g + 1) * M_STATE]
        gcols = slice(g * GROUP_COLS, (g + 1) * GROUP_COLS)
        cb = _dot_nt(cg, bg)
        yoff.append(_dot(cg, h_bf[:, gcols]))
        states.append(_dot_tn(bg, xw_bf[:, gcols]))
        for pr in range(HEADS_PER_GROUP // 2):
            pair = g * (HEADS_PER_GROUP // 2) + pr
            xpair = xs_bf[:, pair * LANES:(pair + 1) * LANES]
            xe, xo = _split_heads(xpair, first)
            rows = []
            for r0, nr, nc in bands:
                ms = [decay_weights(cb, 2 * pair + r, r0, nr, nc) for r in range(2)]
                if q % LANES == 0:
                    rows.append(_dot(jnp.concatenate(ms, axis=1), jnp.concatenate([xe[0:nc], xo[0:nc]], axis=0)))
                else:
                    rows.append(jnp.where(first, _dot(ms[0], xpair), _dot(ms[1], xpair)))
            ydiag.append(rows[0] if len(rows) == 1 else jnp.concatenate(rows, axis=0))
    y = jnp.concatenate(ydiag, axis=1) + jnp.concatenate(yoff, axis=1) * ey_c + dsk_ref[...] * xs
    y = y * zs_ref[b]
    normed = []
    for g in range(M_GROUPS):
        yg = y[:, g * GROUP_COLS:(g + 1) * GROUP_COLS]
        normed.append(yg * lax.rsqrt(jnp.mean(yg * yg, axis=-1, keepdims=True) + EPS))
    y_ref[b] = (jnp.concatenate(normed, axis=1) * mg_ref[...]).astype(BF16)
    h_s[b] = h_prev * dc_c + jnp.concatenate(states, axis=1)


def _ssd(xs, bc, zs, dt, h0, dtb, alog, dsk, mg, ex, *, q):
    bsz, L, _ = xs.shape
    nbat = SSD_BATCHES if bsz % SSD_BATCHES == 0 else 1
    grid = (bsz // nbat, L // q)
    h0p = h0.reshape(bsz, PAIRS, LANES, M_STATE)

    def tok(n):
        return pl.BlockSpec((nbat, q, n), lambda b, t: (b, t, 0))

    per_b = pl.BlockSpec((nbat, PAIRS, LANES, M_STATE), lambda b, t: (b, 0, 0, 0))
    y, nssm = pl.pallas_call(
        functools.partial(_ssd_kernel, q=q, nbat=nbat),
        grid=grid,
        in_specs=[tok(M_INNER), tok(BC_COLS), tok(M_INNER), tok(DT_PAD), per_b,
                  _const_spec((1, DT_PAD)), _const_spec((1, DT_PAD)), _const_spec((1, M_INNER)),
                  _const_spec((1, M_INNER)), _const_spec((DT_PAD, M_INNER))],
        out_specs=(tok(M_INNER), per_b),
        out_shape=(jax.ShapeDtypeStruct((bsz, L, M_INNER), BF16),
                   jax.ShapeDtypeStruct((bsz, PAIRS, LANES, M_STATE), F32)),
        scratch_shapes=[pltpu.VMEM((nbat, M_STATE, M_INNER), F32)],
        compiler_params=pltpu.CompilerParams(
            dimension_semantics=("parallel", "arbitrary"), vmem_limit_bytes=VMEM_LIMIT),
        name="ssd",
    )(xs, bc, zs, dt, h0p, dtb, alog, dsk, mg, ex)
    return y, nssm.reshape(bsz, M_HEADS, M_HEAD_DIM, M_STATE)


def _sb_weights(u, tri, carry, mask):
    drop = jnp.maximum(u, 0.0) + jnp.log2(1.0 + jnp.exp2(-jnp.abs(u)))
    if mask is not None:
        drop = jnp.where(mask, drop, 0.0)
    cs = _dot(drop.astype(BF16), tri)
    w = jnp.exp2(u - cs + carry)
    if mask is not None:
        w = jnp.where(mask, w, 0.0)
    return w.astype(BF16), carry - cs[:, 0:1]


def _suffix_tri(n):
    r = lax.broadcasted_iota(jnp.int32, (n, n), 0)
    c = lax.broadcasted_iota(jnp.int32, (n, n), 1)
    return (r >= c).astype(BF16)


def _strictly_before(n):
    return lax.broadcasted_iota(jnp.int32, (n, n), 1) < lax.broadcasted_iota(jnp.int32, (n, n), 0)


def _live(c_s, nh):
    return jnp.max(functools.reduce(jnp.maximum, [c_s[i] for i in range(nh)]))


SB_STEP_HEADS = 16
PV_GROUP = 4
SB_SAMPLE_HEADS = 8


def _sb_prompt_kernel(q_ref, k_ref, v_ref, tri_ref, o_ref, acc_s, c_s, *, tb):
    qi = pl.program_id(2)
    nh = SB_STEP_HEADS
    width = nh * SB_HEAD_DIM
    first = lax.broadcasted_iota(jnp.int32, (tb, LANES), 1) < SB_HEAD_DIM
    lane_head = lax.broadcasted_iota(jnp.int32, (tb, PV_GROUP * SB_HEAD_DIM), 1) // SB_HEAD_DIM
    tri = tri_ref[...]
    diag_mask = _strictly_before(tb)
    qh = []
    for p in range(nh // 2):
        qh += list(_split_heads(q_ref[0, :, p * LANES:(p + 1) * LANES], first))

    def weights(j, mask, carries):
        rows = pl.ds(pl.multiple_of(j * tb, tb), tb)
        ws, cs = [], []
        for i in range(nh):
            k2 = k_ref[0, rows, (i // 2) * LANES:(i // 2 + 1) * LANES]
            w, c = _sb_weights(_dot_nt(qh[i], k2), tri, carries[i], mask)
            ws.append(w)
            cs.append(c)
        return ws, cs

    def pv(blocks):
        outs = []
        for g in range(nh // PV_GROUP):
            lanes = slice(g * PV_GROUP * SB_HEAD_DIM, (g + 1) * PV_GROUP * SB_HEAD_DIM)
            lhs, rhs = [], []
            for ws, j in blocks:
                v = v_ref[0, pl.ds(pl.multiple_of(j * tb, tb), tb), lanes]
                zero = jnp.zeros_like(v)
                for i in range(PV_GROUP):
                    lhs.append(ws[g * PV_GROUP + i])
                    rhs.append(jnp.where(lane_head == i, v, zero))
            outs.append(_dot(jnp.concatenate(lhs, axis=1), jnp.concatenate(rhs, axis=0)))
        return outs[0] if len(outs) == 1 else jnp.concatenate(outs, axis=1)

    def keep(cs):
        for i in range(nh):
            c_s[i] = cs[i]

    zero = [jnp.zeros((tb, 1), F32)] * nh

    @pl.when(qi == 0)
    def _():
        ws, cs = weights(0, diag_mask, zero)
        acc_s[...] = pv([(ws, 0)])
        keep(cs)

    @pl.when(qi > 0)
    def _():
        ws_d, cs = weights(qi, diag_mask, zero)
        ws_p, cs = weights(qi - 1, None, cs)
        acc_s[...] = pv([(ws_d, qi), (ws_p, qi - 1)])
        keep(cs)

    def cond(st):
        j, m = st
        return jnp.logical_and(j >= 0, m > SB_LOG_CUTOFF)

    def body(st):
        j, _ = st
        ws, cs = weights(j, None, [c_s[i] for i in range(nh)])
        acc_s[...] += pv([(ws, j)])
        keep(cs)
        return j - 1, _live(c_s, nh)

    lax.while_loop(cond, body, (qi - 2, _live(c_s, nh)))
    o_ref[0] = acc_s[...].astype(BF16)


def _sb_prompt(q, k, v, *, tb):
    bsz, L, _ = q.shape
    width = SB_STEP_HEADS * SB_HEAD_DIM
    grid = (bsz, SB_HEADS // SB_STEP_HEADS, L // tb)
    qspec = pl.BlockSpec((1, tb, width), lambda b, h, i: (b, i, h))
    kspec = pl.BlockSpec((1, L, width), lambda b, h, i: (b, 0, h))
    return pl.pallas_call(
        functools.partial(_sb_prompt_kernel, tb=tb),
        grid=grid,
        in_specs=[qspec, kspec, kspec, _const_spec((tb, tb))],
        out_specs=qspec,
        out_shape=jax.ShapeDtypeStruct((bsz, L, SB_WIDTH), BF16),
        scratch_shapes=[pltpu.VMEM((tb, width), F32), pltpu.VMEM((SB_STEP_HEADS, tb, 1), F32)],
        compiler_params=pltpu.CompilerParams(
            dimension_semantics=("parallel", "parallel", "arbitrary"), vmem_limit_bytes=VMEM_LIMIT),
        name="sb_prompt",
    )(q, k, v, _suffix_tri(tb))


def _sb_sample_kernel(q_ref, k_ref, v_ref, pk_ref, pv_ref, tri_d_ref, tri_p_ref, o_ref, acc_s, c_s, *, tq, tk,
                      npast):
    nh = SB_SAMPLE_HEADS
    gw = PV_GROUP * SB_HEAD_DIM
    groups = range(nh // PV_GROUP)
    q2 = q_ref[0]
    first = lax.broadcasted_iota(jnp.int32, (tq, LANES), 1) < SB_HEAD_DIM
    lane_head = lax.broadcasted_iota(jnp.int32, (tq, gw), 1) // SB_HEAD_DIM
    row_head = lax.broadcasted_iota(jnp.int32, (gw, tk), 0) // SB_HEAD_DIM
    tri_p = tri_p_ref[...]
    qh = []
    for p in range(nh // 2):
        qh += list(_split_heads(q2[:, p * LANES:(p + 1) * LANES], first))

    def by_group(ws):
        return [jnp.concatenate(ws[g * PV_GROUP:(g + 1) * PV_GROUP], axis=1) for g in groups]

    def cache_block(j, carries):
        cols = pl.ds(pl.multiple_of(j * tk, tk), tk)
        ws, cs = [], []
        for i in range(nh):
            qs = q2[:, i * SB_HEAD_DIM:(i + 1) * SB_HEAD_DIM]
            w, c = _sb_weights(_dot(qs, pk_ref[0, i, :, cols].astype(BF16)), tri_p, carries[i], None)
            ws.append(w)
            cs.append(c)
        vts = []
        for g in groups:
            vt = pv_ref[0, g * PV_GROUP:(g + 1) * PV_GROUP, :, cols].astype(BF16).reshape(gw, tk)
            zero = jnp.zeros_like(vt)
            vts.append(jnp.concatenate([jnp.where(row_head == i, vt, zero) for i in range(PV_GROUP)], axis=1))
        return by_group(ws), cs, vts

    def lanes(xs):
        return xs[0] if len(xs) == 1 else jnp.concatenate(xs, axis=1)

    ws_d, cs = [], []
    for i in range(nh):
        k2 = k_ref[0, :, (i // 2) * LANES:(i // 2 + 1) * LANES]
        w, c = _sb_weights(_dot_nt(qh[i], k2), tri_d_ref[...], jnp.zeros((tq, 1), F32), _strictly_before(tq))
        ws_d.append(w)
        cs.append(c)
    ws_d = by_group(ws_d)
    ws_p, cs, vts = cache_block(npast - 1, cs)
    outs = []
    for g in groups:
        vnew = v_ref[0, :, g * gw:(g + 1) * gw]
        vals_d = jnp.concatenate([jnp.where(lane_head == i, vnew, jnp.zeros_like(vnew)) for i in range(PV_GROUP)],
                                 axis=0)
        outs.append(_dot(ws_d[g], vals_d) + _dot_nt(ws_p[g], vts[g]))
    acc_s[...] = lanes(outs)
    for i in range(nh):
        c_s[i] = cs[i]

    def cond(st):
        j, m = st
        return jnp.logical_and(j >= 0, m > SB_LOG_CUTOFF)

    def body(st):
        j, _ = st
        ws, cs, vts = cache_block(j, [c_s[i] for i in range(nh)])
        acc_s[...] += lanes([_dot_nt(ws[g], vts[g]) for g in groups])
        for i in range(nh):
            c_s[i] = cs[i]
        return j - 1, _live(c_s, nh)

    lax.while_loop(cond, body, (jnp.int32(npast - 2), _live(c_s, nh)))
    o_ref[0] = acc_s[...].astype(BF16)


def _sb_sample(q, k, v, past_kt, past_vt, *, tk):
    bsz, tq, _ = q.shape
    past = past_kt.shape[3]
    width = SB_SAMPLE_HEADS * SB_HEAD_DIM
    grid = (bsz, SB_HEADS // SB_SAMPLE_HEADS)
    qspec = pl.BlockSpec((1, tq, width), lambda b, h: (b, 0, h))
    pspec = pl.BlockSpec((1, SB_SAMPLE_HEADS, SB_HEAD_DIM, past), lambda b, h: (b, h, 0, 0))
    return pl.pallas_call(
        functools.partial(_sb_sample_kernel, tq=tq, tk=tk, npast=past // tk),
        grid=grid,
        in_specs=[qspec, qspec, qspec, pspec, pspec, _const_spec((tq, tq)), _const_spec((tk, tk))],
        out_specs=qspec,
        out_shape=jax.ShapeDtypeStruct((bsz, tq, SB_WIDTH), BF16),
        scratch_shapes=[pltpu.VMEM((tq, width), F32), pltpu.VMEM((SB_SAMPLE_HEADS, tq, 1), F32)],
        compiler_params=pltpu.CompilerParams(
            dimension_semantics=("parallel", "parallel"), vmem_limit_bytes=VMEM_LIMIT),
        name="sb_sample",
    )(q, k, v, past_kt, past_vt, _suffix_tri(tq), _suffix_tri(tk))


def _post_kernel(x_ref, y_ref, o_ref, ga_ref, gb_ref, p_ref, wa_ref, wb_ref, wo_ref, gf_ref, wg_ref, wu_ref,
                 wd_ref, gp_ref, wpg_ref, wp_ref, gl_ref, out_ref, *, final_norm):
    branch_a = _dot(y_ref[...], wa_ref[...])
    branch_b = _dot(o_ref[...], wb_ref[...])
    merged = jax.nn.sigmoid(ga_ref[...]) * branch_a + jax.nn.sigmoid(gb_ref[...]) * branch_b
    x = x_ref[...] + _dot(merged.astype(BF16), wo_ref[...])
    hf = _rms(x, gf_ref[...]).astype(BF16)
    gate = _dot(hf, wg_ref[...])
    act = (gate * jax.nn.sigmoid(gate) * _dot(hf, wu_ref[...])).astype(BF16)
    x = x + _dot(act, wd_ref[...])
    g = jax.nn.sigmoid(_dot(_rms(x, gp_ref[...]).astype(BF16), wpg_ref[...]))
    x = x + g * _dot(p_ref[...].astype(BF16), wp_ref[...])
    if final_norm:
        x = _rms(x, gl_ref[...])
    out_ref[...] = x


def _post(x, y, o, ga, gb, p, wts, *, tm, final_norm):
    T = x.shape[0]

    def tok(n):
        return pl.BlockSpec((tm, n), lambda i: (i, 0))

    wspecs = [_const_spec(w.shape) for w in wts]
    return pl.pallas_call(
        functools.partial(_post_kernel, final_norm=final_norm),
        grid=(T // tm,),
        in_specs=[tok(D_MODEL), tok(M_INNER), tok(SB_WIDTH), tok(D_MODEL), tok(D_MODEL), tok(PLE_DIM)] + wspecs,
        out_specs=tok(D_MODEL),
        out_shape=jax.ShapeDtypeStruct((T, D_MODEL), F32),
        compiler_params=pltpu.CompilerParams(
            dimension_semantics=("parallel",), vmem_limit_bytes=VMEM_LIMIT),
        name="post",
    )(x, y, o, ga, gb, p, *wts)


def _pack_w_in(w_in):
    w = lax.optimization_barrier(w_in.astype(BF16))
    dt_cols = jnp.pad(w[:, OFF_DT:OFF_Q], ((0, 0), (0, DT_PAD - M_HEADS)))
    parts = [w[:, OFF_Z:OFF_DT], dt_cols, w[:, OFF_Q:OFF_K] * SB_SCALE, w[:, OFF_K:IN_COLS]]
    return jnp.concatenate(parts, axis=1)


def _head_expand():
    r = lax.broadcasted_iota(jnp.int32, (DT_PAD, M_INNER), 0)
    c = lax.broadcasted_iota(jnp.int32, (DT_PAD, M_INNER), 1)
    return (c // M_HEAD_DIM == r).astype(BF16)


def _pad_heads(v):
    return jnp.pad(v.reshape(1, M_HEADS).astype(F32), ((0, 0), (0, DT_PAD - M_HEADS)))


def _layer(x, p, past_k, past_v, conv_hist, h0, prm, *, final_norm):
    bsz, L, _ = x.shape
    prompt = past_k is None
    if prompt:
        nb, tl = 1, min(L, TOKEN_TILE)
        conv_hist = jnp.zeros((bsz, CONV_W - 1, CONV_DIM), F32)
        h0 = jnp.zeros((bsz, M_HEADS, M_HEAD_DIM, M_STATE), F32)
    else:
        nb, tl = bsz, L
    zs, xs, bc, dt, q, k, v, new_k, new_v, ga, gb, new_conv = _inproj(
        x, conv_hist.astype(F32), prm["norm_mix_g"], prm["w_in"], prm["conv_w"], prm["conv_b"],
        nb=nb, tl=tl, kv_feature_major=prompt)
    if prompt:
        new_k, new_v = jnp.swapaxes(new_k, 2, 3), jnp.swapaxes(new_v, 2, 3)

    y, new_ssm = _ssd(xs, bc, zs, dt, h0.astype(F32), prm["dt_bias"], prm["a_log"], prm["d_skip"], prm["mnorm_g"],
                      prm["head_expand"], q=min(L, SSD_CHUNK))
    if prompt:
        o = _sb_prompt(q, k, v, tb=min(L, SB_BLOCK))
    else:
        o = _sb_sample(q, k, v, jnp.swapaxes(past_k.astype(F32), 2, 3), jnp.swapaxes(past_v.astype(F32), 2, 3),
                       tk=min(past_k.shape[2], SB_BLOCK))

    T = bsz * L
    flat = lambda a: a.reshape(T, a.shape[-1])
    x = _post(flat(x), flat(y), flat(o), flat(ga), flat(gb), flat(p), prm["post"], tm=min(T, TOKEN_TILE),
              final_norm=final_norm)
    return x.reshape(bsz, L, D_MODEL), new_k, new_v, new_conv, new_ssm


def _layer_params(i, norm_mix_g, w_in, conv_w, conv_b, dt_bias, a_log, d_skip, mnorm_g, w_a, w_b, w_out,
                  norm_ffn_g, w_gate, w_up, w_down, norm_ple_g, w_ple_gate, w_ple, final_norm_g):
    row = lambda v: v.reshape(1, -1).astype(F32)
    bf = lambda w: w.astype(BF16)
    return {
        "norm_mix_g": row(norm_mix_g[i]),
        "w_in": _pack_w_in(w_in[i]),
        "conv_w": conv_w[i].astype(F32),
        "conv_b": row(conv_b[i]),
        "dt_bias": _pad_heads(dt_bias[i]),
        "a_log": _pad_heads(a_log[i]),
        "d_skip": row(jnp.repeat(d_skip[i], M_HEAD_DIM)),
        "mnorm_g": row(mnorm_g[i]),
        "head_expand": _head_expand(),
        "post": (bf(w_a[i]), bf(w_b[i]), bf(w_out[i]), row(norm_ffn_g[i]), bf(w_gate[i]), bf(w_up[i]),
                 bf(w_down[i]), row(norm_ple_g[i]), bf(w_ple_gate[i]), bf(w_ple[i]), row(final_norm_g)),
    }


def kernel(x_prompt, x_sample, cache_k, cache_v, state_conv, state_ssm, p_prompt, p_sample, norm_mix_g, w_in,
           conv_w, conv_b, dt_bias, a_log, d_skip, mnorm_g, w_a, w_b, w_out, norm_ffn_g, w_gate, w_up, w_down,
           norm_ple_g, w_ple_gate, w_ple, final_norm_g):
    depth = w_in.shape[0]
    xp, xs = x_prompt, x_sample
    outs_p, outs_s = [], []
    for i in range(depth):
        prm = _layer_params(i, norm_mix_g, w_in, conv_w, conv_b, dt_bias, a_log, d_skip, mnorm_g, w_a, w_b,
                            w_out, norm_ffn_g, w_gate, w_up, w_down, norm_ple_g, w_ple_gate, w_ple, final_norm_g)
        last = i == depth - 1
        xp, *st_p = _layer(xp, p_prompt[i], None, None, None, None, prm, final_norm=last)
        xs, *st_s = _layer(xs, p_sample[i], cache_k[i], cache_v[i], state_conv[i], state_ssm[i], prm,
                           final_norm=last)
        outs_p.append(st_p)
        outs_s.append(st_s)
    stack = lambda outs, j: jnp.stack([o[j] for o in outs])
    return (xp, xs,
            stack(outs_p, 0), stack(outs_p, 1), stack(outs_p, 2), stack(outs_p, 3),
            stack(outs_s, 0), stack(outs_s, 1), stack(outs_s, 2), stack(outs_s, 3))
```

```python
import functools

import jax
import jax.numpy as jnp
from jax import lax
from jax.experimental import pallas as pl
from jax.experimental.pallas import tpu as pltpu

F32 = jnp.float32
BF16 = jnp.bfloat16

EPS = 1e-6
D_MODEL = 1024
M_HEAD_DIM = 64
M_HEADS = 16
M_GROUPS = 4
M_STATE = 128
M_INNER = M_HEADS * M_HEAD_DIM
CONV_W = 4
CONV_DIM = M_INNER + 2 * M_GROUPS * M_STATE
SB_HEADS = 16
SB_HEAD_DIM = 64
SB_WIDTH = SB_HEADS * SB_HEAD_DIM
SB_SCALE = SB_HEAD_DIM ** -0.5
PLE_DIM = 256
OFF_Z = 0
OFF_XBC = OFF_Z + M_INNER
OFF_DT = OFF_XBC + CONV_DIM
OFF_Q = OFF_DT + M_HEADS
OFF_K = OFF_Q + SB_WIDTH
OFF_V = OFF_K + SB_WIDTH
OFF_GA = OFF_V + SB_WIDTH
OFF_GB = OFF_GA + D_MODEL
IN_COLS = OFF_GB + D_MODEL

LANES = 128
SUBLANES = 8
DT_PAD = LANES
P_Z = 0
P_XBC = P_Z + M_INNER
P_DT = P_XBC + CONV_DIM
P_Q = P_DT + DT_PAD
P_K = P_Q + SB_WIDTH
P_V = P_K + SB_WIDTH
P_GA = P_V + SB_WIDTH
P_GB = P_GA + D_MODEL
P_COLS = P_GB + D_MODEL

VMEM_LIMIT = 56 * 1024 * 1024
LOG2_E = 1.4426950408889634
SB_LOG_CUTOFF = -150.0
TOKEN_TILE = 256
SB_BLOCK = 256
SSD_CHUNK = 256


def _const_spec(shape):
    nd = len(shape)
    return pl.BlockSpec(shape, lambda *_: (0,) * nd, pipeline_mode=pl.Buffered(1))


def _rms(x, g):
    ms = jnp.mean(x * x, axis=-1, keepdims=True)
    return x * lax.rsqrt(ms + EPS) * g


def _split2(x):
    hi = x.astype(BF16)
    lo = (x - hi.astype(F32)).astype(BF16)
    return hi, lo


def _split3(x):
    hi = x.astype(BF16)
    r = x - hi.astype(F32)
    mid = r.astype(BF16)
    lo = (r - mid.astype(F32)).astype(BF16)
    return hi, mid, lo


def _dot(a, b):
    return jnp.dot(a, b, preferred_element_type=F32)


def _dot_nt(a, b):
    return lax.dot_general(a, b, (((1,), (1,)), ((), ())), preferred_element_type=F32)


def _dot_tn(a, b):
    return lax.dot_general(a, b, (((0,), (0,)), ((), ())), preferred_element_type=F32)


def _split_heads(x2, first):
    zero = jnp.zeros_like(x2)
    return jnp.where(first, x2, zero), jnp.where(first, zero, x2)


HIST_ROW = SUBLANES - (CONV_W - 1)
BC_COLS = 2 * M_GROUPS * M_STATE
CONV_STRIP = 256
PROJ_PIECE = 512


def _inproj_kernel(x_ref, hist_ref, g_ref, w_ref, cw_ref, cb_ref, zs_ref, xs_ref, bc_ref, dt_ref, q_ref, k_ref,
                   v_ref, ko_ref, vo_ref, ga_ref, gb_ref, nconv_ref, xp_s, *, nb, tl, kv_feature_major):
    t = pl.program_id(1)
    tm = nb * tl
    x = x_ref[...].reshape(tm, D_MODEL)
    h = _rms(x, g_ref[...]).astype(BF16)

    def proj(lo, n):
        return _dot(h, w_ref[:, lo:lo + n])

    @pl.when(t == 0)
    def _():
        xp_s[:, HIST_ROW:SUBLANES, :] = hist_ref[...]

    def conv_cols(c0):
        cols = slice(c0, c0 + CONV_STRIP)
        xp_s[:, SUBLANES:SUBLANES + tl, cols] = proj(P_XBC + c0, CONV_STRIP).reshape(nb, tl, CONV_STRIP)
        xp = xp_s[:, :, cols]
        xc = cb_ref[:, cols] + xp[:, SUBLANES:SUBLANES + tl, :] * cw_ref[CONV_W - 1:CONV_W, cols]
        for k in range(1, CONV_W):
            tap = pltpu.roll(xp, k, axis=1)[:, SUBLANES:SUBLANES + tl, :]
            xc = xc + tap * cw_ref[CONV_W - 1 - k:CONV_W - k, cols]
        xc = xc * jax.nn.sigmoid(xc)
        if c0 < M_INNER:
            xs_ref[:, :, cols] = xc
        else:
            bc_ref[:, :, c0 - M_INNER:c0 - M_INNER + CONV_STRIP] = xc.astype(BF16)
        tail = xp_s[:, SUBLANES + tl - (CONV_W - 1):SUBLANES + tl, cols]
        nconv_ref[:, :, cols] = tail
        xp_s[:, HIST_ROW:SUBLANES, cols] = tail

    def half_cols(half):
        return slice(half * PROJ_PIECE, (half + 1) * PROJ_PIECE)

    def kv_proj(src, bf_ref, out_ref, half):
        cols = half_cols(half)
        kv = proj(src + cols.start, PROJ_PIECE)
        bf_ref[:, :, cols] = kv.astype(BF16).reshape(nb, tl, PROJ_PIECE)
        h0, nhd = cols.start // SB_HEAD_DIM, PROJ_PIECE // SB_HEAD_DIM
        if kv_feature_major:
            out_ref[0, h0:h0 + nhd] = kv.T.reshape(nhd, SB_HEAD_DIM, tl)
        else:
            for b in range(nb):
                for hd in range(nhd):
                    out_ref[b, h0 + hd] = kv[b * tl:(b + 1) * tl, hd * SB_HEAD_DIM:(hd + 1) * SB_HEAD_DIM]

    def z_proj(half):
        cols = half_cols(half)
        z = proj(P_Z + cols.start, PROJ_PIECE)
        zs_ref[:, :, cols] = (z * jax.nn.sigmoid(z)).reshape(nb, tl, PROJ_PIECE)

    def q_proj(half):
        cols = half_cols(half)
        if half == 0:
            dt_ref[...] = proj(P_DT, DT_PAD).reshape(nb, tl, DT_PAD)
        q_ref[:, :, cols] = (proj(P_Q + cols.start, PROJ_PIECE) * LOG2_E).astype(BF16).reshape(nb, tl, PROJ_PIECE)

    def gate_proj(src, out_ref, half):
        cols = half_cols(half)
        out_ref[:, :, cols] = proj(src + cols.start, PROJ_PIECE).reshape(nb, tl, PROJ_PIECE)

    others = []
    for fn in (z_proj, q_proj, functools.partial(kv_proj, P_K, k_ref, ko_ref),
               functools.partial(kv_proj, P_V, v_ref, vo_ref), functools.partial(gate_proj, P_GA, ga_ref),
               functools.partial(gate_proj, P_GB, gb_ref)):
        others += [functools.partial(fn, half) for half in range(D_MODEL // PROJ_PIECE)]
    strips = list(range(0, CONV_DIM, CONV_STRIP))
    done = 0
    for i, piece in enumerate(others):
        want = -(-(i + 1) * len(strips) // len(others))
        for c0 in strips[done:want]:
            conv_cols(c0)
        done = max(done, want)
        piece()


def _inproj(x, hist, g, w, cw, cb, *, nb, tl, kv_feature_major):
    bsz, L, _ = x.shape
    grid = (bsz // nb, L // tl)

    def tok(n):
        return pl.BlockSpec((nb, tl, n), lambda b, t: (b, t, 0))

    if kv_feature_major:
        assert nb == 1
        head_shape = (bsz, SB_HEADS, SB_HEAD_DIM, L)
        head = pl.BlockSpec((1, SB_HEADS, SB_HEAD_DIM, tl), lambda b, t: (b, 0, 0, t))
    else:
        head_shape = (bsz, SB_HEADS, L, SB_HEAD_DIM)
        head = pl.BlockSpec((nb, SB_HEADS, tl, SB_HEAD_DIM), lambda b, t: (b, 0, t, 0))
    hist_spec = pl.BlockSpec((nb, CONV_W - 1, CONV_DIM), lambda b, t: (b, 0, 0))
    out_shape = (
        jax.ShapeDtypeStruct((bsz, L, M_INNER), F32),
        jax.ShapeDtypeStruct((bsz, L, M_INNER), F32),
        jax.ShapeDtypeStruct((bsz, L, BC_COLS), BF16),
        jax.ShapeDtypeStruct((bsz, L, DT_PAD), F32),
        jax.ShapeDtypeStruct((bsz, L, SB_WIDTH), BF16),
        jax.ShapeDtypeStruct((bsz, L, SB_WIDTH), BF16),
        jax.ShapeDtypeStruct((bsz, L, SB_WIDTH), BF16),
        jax.ShapeDtypeStruct(head_shape, F32),
        jax.ShapeDtypeStruct(head_shape, F32),
        jax.ShapeDtypeStruct((bsz, L, D_MODEL), F32),
        jax.ShapeDtypeStruct((bsz, L, D_MODEL), F32),
        jax.ShapeDtypeStruct((bsz, CONV_W - 1, CONV_DIM), F32),
    )
    out_specs = (tok(M_INNER), tok(M_INNER), tok(BC_COLS), tok(DT_PAD), tok(SB_WIDTH), tok(SB_WIDTH), tok(SB_WIDTH),
                 head, head, tok(D_MODEL), tok(D_MODEL), hist_spec)
    return pl.pallas_call(
        functools.partial(_inproj_kernel, nb=nb, tl=tl, kv_feature_major=kv_feature_major),
        grid=grid,
        in_specs=[tok(D_MODEL), hist_spec, _const_spec((1, D_MODEL)), _const_spec((D_MODEL, P_COLS)),
                  _const_spec((CONV_W, CONV_DIM)), _const_spec((1, CONV_DIM))],
        out_specs=out_specs,
        out_shape=out_shape,
        scratch_shapes=[pltpu.VMEM((nb, SUBLANES + tl, CONV_DIM), F32)],
        compiler_params=pltpu.CompilerParams(
            dimension_semantics=("parallel", "arbitrary"), vmem_limit_bytes=VMEM_LIMIT),
        name="inproj",
    )(x, hist, g, w, cw, cb)


PAIRS = M_HEADS // 2
HEADS_PER_GROUP = M_HEADS // M_GROUPS
GROUP_COLS = HEADS_PER_GROUP * M_HEAD_DIM
SSD_BATCHES = 4


def _ssd_kernel(xs_ref, bc_ref, zs_ref, dt_ref, *rest, q, nbat, has_state):
    h0_ref = rest[0] if has_state else None
    dtb_ref, alog_ref, dsk_ref, mg_ref, ex_ref, y_ref, nssm_ref, h_s = rest[1:] if has_state else rest
    t = pl.program_id(1)
    nt = pl.num_programs(1)

    @pl.when(t == 0)
    def _():
        if not has_state:
            h_s[...] = jnp.zeros(h_s.shape, F32)
            return
        for b in range(nbat):
            for i in range(PAIRS):
                h_s[b, :, i * LANES:(i + 1) * LANES] = h0_ref[b, i].T

    ri = lax.broadcasted_iota(jnp.int32, (q, q), 0)
    ci = lax.broadcasted_iota(jnp.int32, (q, q), 1)
    causal = ri >= ci
    tri = causal.astype(BF16)
    first = lax.broadcasted_iota(jnp.int32, (q, LANES), 1) < M_HEAD_DIM
    ex = ex_ref[...]

    def head_rows(a):
        if q % LANES:
            a = jnp.concatenate([a, jnp.zeros((LANES - q, LANES), F32)], axis=0)
        return a.T[:, :q]

    for b in range(nbat):
        _ssd_chunk(b, xs_ref, bc_ref, zs_ref, dt_ref, dtb_ref, alog_ref, dsk_ref, mg_ref, y_ref, h_s,
                   q, causal, tri, first, ex, head_rows)

    @pl.when(t == nt - 1)
    def _():
        for b in range(nbat):
            for i in range(PAIRS):
                nssm_ref[b, i] = h_s[b, :, i * LANES:(i + 1) * LANES].T


def _ssd_chunk(b, xs_ref, bc_ref, zs_ref, dt_ref, dtb_ref, alog_ref, dsk_ref, mg_ref, y_ref, h_s,
               q, causal, tri, first, ex, head_rows):
    xs = xs_ref[b]
    bm = bc_ref[b, :, 0:M_GROUPS * M_STATE]
    cm = bc_ref[b, :, M_GROUPS * M_STATE:BC_COLS]
    xdt = dt_ref[b] + dtb_ref[...]
    dt = jnp.maximum(xdt, 0.0) + jnp.log1p(jnp.exp(-jnp.abs(xdt)))
    da = dt * (-jnp.exp(alog_ref[...]))

    acum = sum(_dot(tri, p) for p in _split3(da))
    tot = acum[q - 1:q, :]
    ey = jnp.exp(acum)
    ws = jnp.exp(tot - acum) * dt
    dc = jnp.broadcast_to(jnp.exp(tot), (SUBLANES, DT_PAD))
    stack_c = _dot(jnp.concatenate([ey, ws], axis=0).astype(BF16), ex)
    ey_c = stack_c[0:q]
    ws_c = stack_c[q:2 * q]
    dc_c = sum(_dot(p, ex) for p in _split2(dc))[0:1]
    acum2 = acum * LOG2_E
    acum2_t = head_rows(acum2)
    dt_t = head_rows(dt)

    def decay_weights(cb, hd, r0, nr, nc):
        acol = jnp.broadcast_to(acum2[r0:r0 + nr, hd:hd + 1], (nr, nc))
        arow = jnp.broadcast_to(acum2_t[hd:hd + 1, 0:nc], (nr, nc))
        dec = jnp.exp2(jnp.where(causal[r0:r0 + nr, 0:nc], acol - arow, -1e30))
        return (cb[r0:r0 + nr, 0:nc] * dec * jnp.broadcast_to(dt_t[hd:hd + 1, 0:nc], (nr, nc))).astype(BF16)

    bands = ((0, q // 2, q // 2), (q // 2, q // 2, q)) if q % (2 * LANES) == 0 else ((0, q, q),)

    xs_bf = xs.astype(BF16)
    xw_bf = (xs * ws_c).astype(BF16)
    h_prev = h_s[b]
    h_bf = h_prev.astype(BF16)

    ydiag, yoff, states = [], [], []
    for g in range(M_GROUPS):
        bg = bm[:, g * M_STATE:(g + 1) * M_STATE]
        cg = cm[:, g * M_STATE:(g + 1) * M_STATE]
        gcols = slice(g * GROUP_COLS, (g + 1) * GROUP_COLS)
        cb = _dot_nt(cg, bg)
        yoff.append(_dot(cg, h_bf[:, gcols]))
        states.append(_dot_tn(bg, xw_bf[:, gcols]))
        for pr in range(HEADS_PER_GROUP // 2):
            pair = g * (HEADS_PER_GROUP // 2) + pr
            xpair = xs_bf[:, pair * LANES:(pair + 1) * LANES]
            xe, xo = _split_heads(xpair, first)
            rows = []
            for r0, nr, nc in bands:
                ms = [decay_weights(cb, 2 * pair + r, r0, nr, nc) for r in range(2)]
                if q % LANES == 0:
                    rows.append(_dot(jnp.concatenate(ms, axis=1), jnp.concatenate([xe[0:nc], xo[0:nc]], axis=0)))
                else:
                    rows.append(jnp.where(first, _dot(ms[0], xpair), _dot(ms[1], xpair)))
            ydiag.append(rows[0] if len(rows) == 1 else jnp.concatenate(rows, axis=0))
    y = jnp.concatenate(ydiag, axis=1) + jnp.concatenate(yoff, axis=1) * ey_c + dsk_ref[...] * xs
    y = y * zs_ref[b]
    normed = []
    for g in range(M_GROUPS):
        yg = y[:, g * GROUP_COLS:(g + 1) * GROUP_COLS]
        normed.append(yg * lax.rsqrt(jnp.mean(yg * yg, axis=-1, keepdims=True) + EPS))
    y_ref[b] = (jnp.concatenate(normed, axis=1) * mg_ref[...]).astype(BF16)
    h_s[b] = h_prev * dc_c + jnp.concatenate(states, axis=1)


def _ssd(xs, bc, zs, dt, h0, dtb, alog, dsk, mg, ex, *, q):
    bsz, L, _ = xs.shape
    nbat = SSD_BATCHES if bsz % SSD_BATCHES == 0 else 1
    grid = (bsz // nbat, L // q)
    has_state = h0 is not None

    def tok(n):
        return pl.BlockSpec((nbat, q, n), lambda b, t: (b, t, 0))

    per_b = pl.BlockSpec((nbat, PAIRS, LANES, M_STATE), lambda b, t: (b, 0, 0, 0))
    state = [h0.reshape(bsz, PAIRS, LANES, M_STATE)] if has_state else []
    y, nssm = pl.pallas_call(
        functools.partial(_ssd_kernel, q=q, nbat=nbat, has_state=has_state),
        grid=grid,
        in_specs=[tok(M_INNER), tok(BC_COLS), tok(M_INNER), tok(DT_PAD)] + [per_b] * has_state +
                 [_const_spec((1, DT_PAD)), _const_spec((1, DT_PAD)), _const_spec((1, M_INNER)),
                  _const_spec((1, M_INNER)), _const_spec((DT_PAD, M_INNER))],
        out_specs=(tok(M_INNER), per_b),
        out_shape=(jax.ShapeDtypeStruct((bsz, L, M_INNER), BF16),
                   jax.ShapeDtypeStruct((bsz, PAIRS, LANES, M_STATE), F32)),
        scratch_shapes=[pltpu.VMEM((nbat, M_STATE, M_INNER), F32)],
        compiler_params=pltpu.CompilerParams(
            dimension_semantics=("parallel", "arbitrary"), vmem_limit_bytes=VMEM_LIMIT),
        name="ssd",
    )(xs, bc, zs, dt, *state, dtb, alog, dsk, mg, ex)
    return y, nssm.reshape(bsz, M_HEADS, M_HEAD_DIM, M_STATE)


def _sb_weights(u, tri, carry, mask):
    drop = jnp.maximum(u, 0.0) + jnp.log2(1.0 + jnp.exp2(-jnp.abs(u)))
    if mask is not None:
        drop = jnp.where(mask, drop, 0.0)
    cs = _dot(drop.astype(BF16), tri)
    w = jnp.exp2(u - cs + carry)
    if mask is not None:
        w = jnp.where(mask, w, 0.0)
    return w.astype(BF16), carry - cs[:, 0:1]


def _suffix_tri(n):
    r = lax.broadcasted_iota(jnp.int32, (n, n), 0)
    c = lax.broadcasted_iota(jnp.int32, (n, n), 1)
    return (r >= c).astype(BF16)


def _strictly_before(n):
    return lax.broadcasted_iota(jnp.int32, (n, n), 1) < lax.broadcasted_iota(jnp.int32, (n, n), 0)


def _live(c_s, nh):
    return jnp.max(functools.reduce(jnp.maximum, [c_s[i] for i in range(nh)]))


SB_STEP_HEADS = 16
PV_GROUP = 4
SB_SAMPLE_HEADS = 8


def _sb_prompt_kernel(q_ref, k_ref, v_ref, tri_ref, o_ref, acc_s, c_s, *, tb):
    qi = pl.program_id(2)
    nh = SB_STEP_HEADS
    width = nh * SB_HEAD_DIM
    first = lax.broadcasted_iota(jnp.int32, (tb, LANES), 1) < SB_HEAD_DIM
    lane_head = lax.broadcasted_iota(jnp.int32, (tb, PV_GROUP * SB_HEAD_DIM), 1) // SB_HEAD_DIM
    tri = tri_ref[...]
    diag_mask = _strictly_before(tb)
    qh = []
    for p in range(nh // 2):
        qh += list(_split_heads(q_ref[0, :, p * LANES:(p + 1) * LANES], first))

    def weights(j, mask, carries):
        rows = pl.ds(pl.multiple_of(j * tb, tb), tb)
        ws, cs = [], []
        for i in range(nh):
            k2 = k_ref[0, rows, (i // 2) * LANES:(i // 2 + 1) * LANES]
            w, c = _sb_weights(_dot_nt(qh[i], k2), tri, carries[i], mask)
            ws.append(w)
            cs.append(c)
        return ws, cs

    def pv(blocks):
        outs = []
        for g in range(nh // PV_GROUP):
            lanes = slice(g * PV_GROUP * SB_HEAD_DIM, (g + 1) * PV_GROUP * SB_HEAD_DIM)
            lhs, rhs = [], []
            for ws, j in blocks:
                v = v_ref[0, pl.ds(pl.multiple_of(j * tb, tb), tb), lanes]
                zero = jnp.zeros_like(v)
                for i in range(PV_GROUP):
                    lhs.append(ws[g * PV_GROUP + i])
                    rhs.append(jnp.where(lane_head == i, v, zero))
            outs.append(_dot(jnp.concatenate(lhs, axis=1), jnp.concatenate(rhs, axis=0)))
        return outs[0] if len(outs) == 1 else jnp.concatenate(outs, axis=1)

    def keep(cs):
        for i in range(nh):
            c_s[i] = cs[i]

    zero = [jnp.zeros((tb, 1), F32)] * nh

    @pl.when(qi == 0)
    def _():
        ws, cs = weights(0, diag_mask, zero)
        acc_s[...] = pv([(ws, 0)])
        keep(cs)

    @pl.when(qi > 0)
    def _():
        ws_d, cs = weights(qi, diag_mask, zero)
        ws_p, cs = weights(qi - 1, None, cs)
        acc_s[...] = pv([(ws_d, qi), (ws_p, qi - 1)])
        keep(cs)

    def cond(st):
        j, m = st
        return jnp.logical_and(j >= 0, m > SB_LOG_CUTOFF)

    def body(st):
        j, _ = st
        ws, cs = weights(j, None, [c_s[i] for i in range(nh)])
        acc_s[...] += pv([(ws, j)])
        keep(cs)
        return j - 1, _live(c_s, nh)

    lax.while_loop(cond, body, (qi - 2, _live(c_s, nh)))
    o_ref[0] = acc_s[...].astype(BF16)


def _sb_prompt(q, k, v, *, tb):
    bsz, L, _ = q.shape
    width = SB_STEP_HEADS * SB_HEAD_DIM
    grid = (bsz, SB_HEADS // SB_STEP_HEADS, L // tb)
    qspec = pl.BlockSpec((1, tb, width), lambda b, h, i: (b, i, h))
    kspec = pl.BlockSpec((1, L, width), lambda b, h, i: (b, 0, h))
    return pl.pallas_call(
        functools.partial(_sb_prompt_kernel, tb=tb),
        grid=grid,
        in_specs=[qspec, kspec, kspec, _const_spec((tb, tb))],
        out_specs=qspec,
        out_shape=jax.ShapeDtypeStruct((bsz, L, SB_WIDTH), BF16),
        scratch_shapes=[pltpu.VMEM((tb, width), F32), pltpu.VMEM((SB_STEP_HEADS, tb, 1), F32)],
        compiler_params=pltpu.CompilerParams(
            dimension_semantics=("parallel", "parallel", "arbitrary"), vmem_limit_bytes=VMEM_LIMIT),
        name="sb_prompt",
    )(q, k, v, _suffix_tri(tb))


def _sb_sample_kernel(q_ref, k_ref, v_ref, pk_ref, pv_ref, tri_d_ref, tri_p_ref, o_ref, acc_s, c_s, *, tq, tk,
                      npast):
    nh = SB_SAMPLE_HEADS
    gw = PV_GROUP * SB_HEAD_DIM
    groups = range(nh // PV_GROUP)
    q2 = q_ref[0]
    first = lax.broadcasted_iota(jnp.int32, (tq, LANES), 1) < SB_HEAD_DIM
    lane_head = lax.broadcasted_iota(jnp.int32, (tq, gw), 1) // SB_HEAD_DIM
    row_head = lax.broadcasted_iota(jnp.int32, (gw, tk), 0) // SB_HEAD_DIM
    tri_p = tri_p_ref[...]
    qh = []
    for p in range(nh // 2):
        qh += list(_split_heads(q2[:, p * LANES:(p + 1) * LANES], first))

    def by_group(ws):
        return [jnp.concatenate(ws[g * PV_GROUP:(g + 1) * PV_GROUP], axis=1) for g in groups]

    def cache_block(j, carries):
        cols = pl.ds(pl.multiple_of(j * tk, tk), tk)
        ws, cs = [], []
        for i in range(nh):
            qs = q2[:, i * SB_HEAD_DIM:(i + 1) * SB_HEAD_DIM]
            w, c = _sb_weights(_dot(qs, pk_ref[0, i, :, cols].astype(BF16)), tri_p, carries[i], None)
            ws.append(w)
            cs.append(c)
        vts = []
        for g in groups:
            vt = pv_ref[0, g * PV_GROUP:(g + 1) * PV_GROUP, :, cols].astype(BF16).reshape(gw, tk)
            zero = jnp.zeros_like(vt)
            vts.append(jnp.concatenate([jnp.where(row_head == i, vt, zero) for i in range(PV_GROUP)], axis=1))
        return by_group(ws), cs, vts

    def lanes(xs):
        return xs[0] if len(xs) == 1 else jnp.concatenate(xs, axis=1)

    ws_d, cs = [], []
    for i in range(nh):
        k2 = k_ref[0, :, (i // 2) * LANES:(i // 2 + 1) * LANES]
        w, c = _sb_weights(_dot_nt(qh[i], k2), tri_d_ref[...], jnp.zeros((tq, 1), F32), _strictly_before(tq))
        ws_d.append(w)
        cs.append(c)
    ws_d = by_group(ws_d)
    ws_p, cs, vts = cache_block(npast - 1, cs)
    outs = []
    for g in groups:
        vnew = v_ref[0, :, g * gw:(g + 1) * gw]
        vals_d = jnp.concatenate([jnp.where(lane_head == i, vnew, jnp.zeros_like(vnew)) for i in range(PV_GROUP)],
                                 axis=0)
        outs.append(_dot(ws_d[g], vals_d) + _dot_nt(ws_p[g], vts[g]))
    acc_s[...] = lanes(outs)
    for i in range(nh):
        c_s[i] = cs[i]

    def cond(st):
        j, m = st
        return jnp.logical_and(j >= 0, m > SB_LOG_CUTOFF)

    def body(st):
        j, _ = st
        ws, cs, vts = cache_block(j, [c_s[i] for i in range(nh)])
        acc_s[...] += lanes([_dot_nt(ws[g], vts[g]) for g in groups])
        for i in range(nh):
            c_s[i] = cs[i]
        return j - 1, _live(c_s, nh)

    lax.while_loop(cond, body, (jnp.int32(npast - 2), _live(c_s, nh)))
    o_ref[0] = acc_s[...].astype(BF16)


def _sb_sample(q, k, v, past_kt, past_vt, *, tk):
    bsz, tq, _ = q.shape
    past = past_kt.shape[3]
    width = SB_SAMPLE_HEADS * SB_HEAD_DIM
    grid = (bsz, SB_HEADS // SB_SAMPLE_HEADS)
    qspec = pl.BlockSpec((1, tq, width), lambda b, h: (b, 0, h))
    pspec = pl.BlockSpec((1, SB_SAMPLE_HEADS, SB_HEAD_DIM, past), lambda b, h: (b, h, 0, 0))
    return pl.pallas_call(
        functools.partial(_sb_sample_kernel, tq=tq, tk=tk, npast=past // tk),
        grid=grid,
        in_specs=[qspec, qspec, qspec, pspec, pspec, _const_spec((tq, tq)), _const_spec((tk, tk))],
        out_specs=qspec,
        out_shape=jax.ShapeDtypeStruct((bsz, tq, SB_WIDTH), BF16),
        scratch_shapes=[pltpu.VMEM((tq, width), F32), pltpu.VMEM((SB_SAMPLE_HEADS, tq, 1), F32)],
        compiler_params=pltpu.CompilerParams(
            dimension_semantics=("parallel", "parallel"), vmem_limit_bytes=VMEM_LIMIT),
        name="sb_sample",
    )(q, k, v, past_kt, past_vt, _suffix_tri(tq), _suffix_tri(tk))


def _post_kernel(x_ref, y_ref, o_ref, ga_ref, gb_ref, p_ref, wa_ref, wb_ref, wo_ref, gf_ref, wg_ref, wu_ref,
                 wd_ref, gp_ref, wpg_ref, wp_ref, gl_ref, out_ref, *, final_norm):
    branch_a = _dot(y_ref[...], wa_ref[...])
    branch_b = _dot(o_ref[...], wb_ref[...])
    merged = jax.nn.sigmoid(ga_ref[...]) * branch_a + jax.nn.sigmoid(gb_ref[...]) * branch_b
    x = x_ref[...] + _dot(merged.astype(BF16), wo_ref[...])
    hf = _rms(x, gf_ref[...]).astype(BF16)
    gate = _dot(hf, wg_ref[...])
    act = (gate * jax.nn.sigmoid(gate) * _dot(hf, wu_ref[...])).astype(BF16)
    x = x + _dot(act, wd_ref[...])
    g = jax.nn.sigmoid(_dot(_rms(x, gp_ref[...]).astype(BF16), wpg_ref[...]))
    x = x + g * _dot(p_ref[...].astype(BF16), wp_ref[...])
    if final_norm:
        x = _rms(x, gl_ref[...])
    out_ref[...] = x


def _post(x, y, o, ga, gb, p, wts, *, tm, final_norm):
    T = x.shape[0]

    def tok(n):
        return pl.BlockSpec((tm, n), lambda i: (i, 0))

    wspecs = [_const_spec(w.shape) for w in wts]
    return pl.pallas_call(
        functools.partial(_post_kernel, final_norm=final_norm),
        grid=(T // tm,),
        in_specs=[tok(D_MODEL), tok(M_INNER), tok(SB_WIDTH), tok(D_MODEL), tok(D_MODEL), tok(PLE_DIM)] + wspecs,
        out_specs=tok(D_MODEL),
        out_shape=jax.ShapeDtypeStruct((T, D_MODEL), F32),
        compiler_params=pltpu.CompilerParams(
            dimension_semantics=("parallel",), vmem_limit_bytes=VMEM_LIMIT),
        name="post",
    )(x, y, o, ga, gb, p, *wts)


def _pack_w_in(w_in):
    w = lax.optimization_barrier(w_in.astype(BF16))
    dt_cols = jnp.pad(w[:, OFF_DT:OFF_Q], ((0, 0), (0, DT_PAD - M_HEADS)))
    parts = [w[:, OFF_Z:OFF_DT], dt_cols, w[:, OFF_Q:OFF_K] * SB_SCALE, w[:, OFF_K:IN_COLS]]
    return jnp.concatenate(parts, axis=1)


def _head_expand():
    r = lax.broadcasted_iota(jnp.int32, (DT_PAD, M_INNER), 0)
    c = lax.broadcasted_iota(jnp.int32, (DT_PAD, M_INNER), 1)
    return (c // M_HEAD_DIM == r).astype(BF16)


def _pad_heads(v):
    return jnp.pad(v.reshape(1, M_HEADS).astype(F32), ((0, 0), (0, DT_PAD - M_HEADS)))


def _layer(x, p, past_k, past_v, conv_hist, h0, prm, *, final_norm):
    bsz, L, _ = x.shape
    prompt = past_k is None
    if prompt:
        nb, tl = 1, min(L, TOKEN_TILE)
        conv_hist = jnp.zeros((bsz, CONV_W - 1, CONV_DIM), F32)
    else:
        nb, tl = bsz, L
    zs, xs, bc, dt, q, k, v, new_k, new_v, ga, gb, new_conv = _inproj(
        x, conv_hist.astype(F32), prm["norm_mix_g"], prm["w_in"], prm["conv_w"], prm["conv_b"],
        nb=nb, tl=tl, kv_feature_major=prompt)
    if prompt:
        new_k, new_v = jnp.swapaxes(new_k, 2, 3), jnp.swapaxes(new_v, 2, 3)

    y, new_ssm = _ssd(xs, bc, zs, dt, None if h0 is None else h0.astype(F32), prm["dt_bias"], prm["a_log"], prm["d_skip"], prm["mnorm_g"],
                      prm["head_expand"], q=min(L, SSD_CHUNK))
    if prompt:
        o = _sb_prompt(q, k, v, tb=min(L, SB_BLOCK))
    else:
        o = _sb_sample(q, k, v, jnp.swapaxes(past_k.astype(F32), 2, 3), jnp.swapaxes(past_v.astype(F32), 2, 3),
                       tk=min(past_k.shape[2], SB_BLOCK))

    T = bsz * L
    flat = lambda a: a.reshape(T, a.shape[-1])
    x = _post(flat(x), flat(y), flat(o), flat(ga), flat(gb), flat(p), prm["post"], tm=min(T, TOKEN_TILE),
              final_norm=final_norm)
    return x.reshape(bsz, L, D_MODEL), new_k, new_v, new_conv, new_ssm


def _layer_params(i, norm_mix_g, w_in, conv_w, conv_b, dt_bias, a_log, d_skip, mnorm_g, w_a, w_b, w_out,
                  norm_ffn_g, w_gate, w_up, w_down, norm_ple_g, w_ple_gate, w_ple, final_norm_g):
    row = lambda v: v.reshape(1, -1).astype(F32)
    bf = lambda w: w.astype(BF16)
    return {
        "norm_mix_g": row(norm_mix_g[i]),
        "w_in": _pack_w_in(w_in[i]),
        "conv_w": conv_w[i].astype(F32),
        "conv_b": row(conv_b[i]),
        "dt_bias": _pad_heads(dt_bias[i]),
        "a_log": _pad_heads(a_log[i]),
        "d_skip": row(jnp.repeat(d_skip[i], M_HEAD_DIM)),
        "mnorm_g": row(mnorm_g[i]),
        "head_expand": _head_expand(),
        "post": (bf(w_a[i]), bf(w_b[i]), bf(w_out[i]), row(norm_ffn_g[i]), bf(w_gate[i]), bf(w_up[i]),
                 bf(w_down[i]), row(norm_ple_g[i]), bf(w_ple_gate[i]), bf(w_ple[i]), row(final_norm_g)),
    }


def kernel(x_prompt, x_sample, cache_k, cache_v, state_conv, state_ssm, p_prompt, p_sample, norm_mix_g, w_in,
           conv_w, conv_b, dt_bias, a_log, d_skip, mnorm_g, w_a, w_b, w_out, norm_ffn_g, w_gate, w_up, w_down,
           norm_ple_g, w_ple_gate, w_ple, final_norm_g):
    depth = w_in.shape[0]
    xp, xs = x_prompt, x_sample
    outs_p, outs_s = [], []
    for i in range(depth):
        prm = _layer_params(i, norm_mix_g, w_in, conv_w, conv_b, dt_bias, a_log, d_skip, mnorm_g, w_a, w_b,
                            w_out, norm_ffn_g, w_gate, w_up, w_down, norm_ple_g, w_ple_gate, w_ple, final_norm_g)
        last = i == depth - 1
        xp, *st_p = _layer(xp, p_prompt[i], None, None, None, None, prm, final_norm=last)
        xs, *st_s = _layer(xs, p_sample[i], cache_k[i], cache_v[i], state_conv[i], state_ssm[i], prm,
                           final_norm=last)
        outs_p.append(st_p)
        outs_s.append(st_s)
    stack = lambda outs, j: jnp.stack([o[j] for o in outs])
    return (xp, xs,
            stack(outs_p, 0), stack(outs_p, 1), stack(outs_p, 2), stack(outs_p, 3),
            stack(outs_s, 0), stack(outs_s, 1), stack(outs_s, 2), stack(outs_s, 3))
```
